```python
import jax, jax.numpy as jnp
from jax import lax
import numpy as np

D_MODEL = 2048
BATCH = 1
SEQ = 16384
DEPTH = 1

N_ATTN_HEADS = 8
HEAD_DIM = 128
ATTN_WIDTH = N_ATTN_HEADS * HEAD_DIM
N_SGU_GROUPS = 8
SGU_GROUP_DIM = 128
SGU_WIDTH = N_SGU_GROUPS * SGU_GROUP_DIM
MIX_WIDTH = ATTN_WIDTH + SGU_WIDTH
IN_COLS = 3 * ATTN_WIDTH + 2 * SGU_WIDTH
MOBA_BLOCK = 256
MOBA_TOPK = 3
Q_BLOCK = 128
ROPE_THETA = 500000.0
ROPE_DIM = HEAD_DIM // 4
SGU_CHUNK = 128
MEM_LEN = 256
MEM_HEADS = 4
MEM_HEAD_DIM = 128
N_EXPERTS = 256
TOP_K = 8
N_EXPERT_GROUPS = 8
TOPK_GROUPS = 4
EXPERT_DIM = 512
SHARED_DIM = 512
ROUTED_SCALE = 2.5
MOE_BLOCK = 128
LN_EPS = 1e-5
DEEPNORM_ALPHA = (2 * DEPTH) ** 0.25
DEEPNORM_BETA = (8 * DEPTH) ** -0.25

kernel_name = 'hybrid_moba_gmlp_memxattn_moe_deepnorm'


def layer_norm(x, g, b):
    xf = x.astype(jnp.float32)
    mu = xf.mean(-1, keepdims=True)
    var = jnp.square(xf - mu).mean(-1, keepdims=True)
    y = (xf - mu) * lax.rsqrt(var + LN_EPS) * g.astype(jnp.float32) + b.astype(jnp.float32)
    return y.astype(x.dtype)


def partial_rope(t, pos):
    half = ROPE_DIM // 2
    inv_freq = ROPE_THETA ** (-jnp.arange(half, dtype=jnp.float32) / half)
    ang = pos.astype(jnp.float32)[:, None] * inv_freq[None, :]
    cos, sin = jnp.cos(ang), jnp.sin(ang)
    tr = t[..., :ROPE_DIM].astype(jnp.float32)
    t1, t2 = tr[..., :half], tr[..., half:]
    rot = jnp.concatenate([t1 * cos - t2 * sin, t2 * cos + t1 * sin], axis=-1).astype(t.dtype)
    return jnp.concatenate([rot, t[..., ROPE_DIM:]], axis=-1)


def moba_attention(q, k, v):
    b, h, s, d = q.shape
    s_pad = -(-s // MOBA_BLOCK) * MOBA_BLOCK
    pad = ((0, 0), (0, 0), (0, s_pad - s), (0, 0))
    q = jnp.pad(q, pad)
    k = jnp.pad(k, pad)
    v = jnp.pad(v, pad)
    nb = s_pad // MOBA_BLOCK
    kk = min(MOBA_TOPK, nb)
    nq = s_pad // Q_BLOCK
    scale = d ** -0.5
    kb = k.reshape(b, h, nb, MOBA_BLOCK, d)
    vb = v.reshape(b, h, nb, MOBA_BLOCK, d)
    k_mean = kb.astype(jnp.float32).mean(axis=3)
    bi = jnp.arange(b)[:, None, None, None]
    hi = jnp.arange(h)[None, :, None, None]
    blk_ids = jnp.arange(nb)
    key_off = jnp.arange(MOBA_BLOCK)
    q_off = jnp.arange(Q_BLOCK)

    def query_block(args):
        qc, c = args
        own = (c * Q_BLOCK) // MOBA_BLOCK
        q_pos = c * Q_BLOCK + q_off
        gate = jnp.einsum('bhqd,bhnd->bhqn', qc.astype(jnp.float32), k_mean)
        gate = jnp.where(blk_ids < own, gate, -jnp.inf)
        _, sel = lax.top_k(gate, kk)
        sel_ok = sel < own
        k_sel = kb[bi, hi, sel]
        v_sel = vb[bi, hi, sel]
        s_sel = jnp.einsum('bhqd,bhqjkd->bhqjk', qc, k_sel,
                           preferred_element_type=jnp.float32) * scale
        s_sel = jnp.where(sel_ok[..., None], s_sel, -jnp.inf)
        k_own = lax.dynamic_index_in_dim(kb, own, axis=2, keepdims=False)
        v_own = lax.dynamic_index_in_dim(vb, own, axis=2, keepdims=False)
        s_own = jnp.einsum('bhqd,bhkd->bhqk', qc, k_own,
                           preferred_element_type=jnp.float32) * scale
        k_pos = own * MOBA_BLOCK + key_off
        s_own = jnp.where(k_pos[None, :] <= q_pos[:, None], s_own, -jnp.inf)
        scores = jnp.concatenate([s_sel.reshape(b, h, Q_BLOCK, kk * MOBA_BLOCK), s_own], axis=-1)
        p = jax.nn.softmax(scores, axis=-1)
        p_sel = p[..., :kk * MOBA_BLOCK].reshape(b, h, Q_BLOCK, kk, MOBA_BLOCK).astype(v.dtype)
        p_own = p[..., kk * MOBA_BLOCK:].astype(v.dtype)
        o = (jnp.einsum('bhqjk,bhqjkd->bhqd', p_sel, v_sel, preferred_element_type=jnp.float32)
             + jnp.einsum('bhqk,bhkd->bhqd', p_own, v_own, preferred_element_type=jnp.float32))
        return o.astype(q.dtype)

    q_blocks = q.reshape(b, h, nq, Q_BLOCK, d).transpose(2, 0, 1, 3, 4)
    out = lax.map(query_block, (q_blocks, jnp.arange(nq)))
    return out.transpose(1, 2, 0, 3, 4).reshape(b, h, s_pad, d)[:, :, :s]


def spatial_gating(u, z, ln_g, ln_b, w_s, b_s):
    b, s, g, c = z.shape
    z = layer_norm(z, ln_g, ln_b)
    nc = s // SGU_CHUNK
    zc = z.reshape(b, nc, SGU_CHUNK, g, c)
    causal = jnp.tril(jnp.ones((SGU_CHUNK, SGU_CHUNK), dtype=bool))
    w = jnp.where(causal[None], w_s, 0).astype(z.dtype)
    mixed = jnp.einsum('gts,bnsgc->bntgc', w, zc) + b_s.T[None, None, :, :, None].astype(z.dtype)
    return u * mixed.reshape(b, s, g, c)


def hybrid_mixer(x, w_in, sgu_ln_g, sgu_ln_b, w_s, b_s, w_out):
    b, s, _ = x.shape
    proj = jnp.einsum('bsd,de->bse', x, w_in)
    q, k, v, u, z = jnp.split(proj, [ATTN_WIDTH, 2 * ATTN_WIDTH, 3 * ATTN_WIDTH,
                                     3 * ATTN_WIDTH + SGU_WIDTH], axis=-1)

    def heads(t):
        return t.reshape(b, s, N_ATTN_HEADS, HEAD_DIM).transpose(0, 2, 1, 3)

    pos = jnp.arange(s)
    attn = moba_attention(partial_rope(heads(q), pos), partial_rope(heads(k), pos), heads(v))
    attn = attn.transpose(0, 2, 1, 3).reshape(b, s, ATTN_WIDTH)
    u = jax.nn.gelu(u).reshape(b, s, N_SGU_GROUPS, SGU_GROUP_DIM)
    z = jax.nn.gelu(z).reshape(b, s, N_SGU_GROUPS, SGU_GROUP_DIM)
    sgu = spatial_gating(u, z, sgu_ln_g, sgu_ln_b, w_s, b_s).reshape(b, s, SGU_WIDTH)
    return jnp.einsum('bse,ed->bsd', jnp.concatenate([attn, sgu], axis=-1), w_out)


def memory_cross_attention(x, mem, w_mq, w_mkv, w_mo):
    b, s, _ = x.shape
    m = mem.shape[1]
    q = jnp.einsum('bsd,de->bse', x, w_mq).reshape(b, s, MEM_HEADS, MEM_HEAD_DIM)
    kv = jnp.einsum('bmd,de->bme', mem, w_mkv)
    k, v = jnp.split(kv, 2, axis=-1)
    k = k.reshape(b, m, MEM_HEADS, MEM_HEAD_DIM)
    v = v.reshape(b, m, MEM_HEADS, MEM_HEAD_DIM)
    scores = jnp.einsum('bshd,bmhd->bhsm', q, k, preferred_element_type=jnp.float32) * (MEM_HEAD_DIM ** -0.5)
    p = jax.nn.softmax(scores, axis=-1).astype(v.dtype)
    o = jnp.einsum('bhsm,bmhd->bshd', p, v).reshape(b, s, MEM_HEADS * MEM_HEAD_DIM)
    return jnp.einsum('bse,ed->bsd', o, w_mo)


def moe_ffn(x, w_router, router_bias, w1, w3, w2, ws1, ws3, ws2):
    b, s, d = x.shape
    t = b * s
    xf = x.reshape(t, d)
    logits = jnp.einsum('td,de->te', xf, w_router, preferred_element_type=jnp.float32)
    scores = jax.nn.sigmoid(logits)
    biased = scores + router_bias.astype(jnp.float32)
    grp = biased.reshape(t, N_EXPERT_GROUPS, N_EXPERTS // N_EXPERT_GROUPS)
    grp_score = lax.top_k(grp, 2)[0].sum(-1)
    _, grp_idx = lax.top_k(grp_score, TOPK_GROUPS)
    grp_mask = jnp.any(grp_idx[..., None] == jnp.arange(N_EXPERT_GROUPS), axis=-2)
    expert_mask = jnp.repeat(grp_mask, N_EXPERTS // N_EXPERT_GROUPS, axis=-1)
    _, eidx = lax.top_k(jnp.where(expert_mask, biased, -jnp.inf), TOP_K)
    gw = jnp.take_along_axis(scores, eidx, axis=-1)
    gw = gw / gw.sum(-1, keepdims=True) * ROUTED_SCALE

    m = t * TOP_K
    flat_e = eidx.reshape(m)
    flat_t = jnp.repeat(jnp.arange(t, dtype=jnp.int32), TOP_K)
    flat_w = gw.reshape(m)
    order = jnp.argsort(flat_e)
    se, st, sw = flat_e[order], flat_t[order], flat_w[order]
    counts = jnp.bincount(flat_e, length=N_EXPERTS)
    starts = jnp.cumsum(counts) - counts
    pcounts = (counts + MOE_BLOCK - 1) // MOE_BLOCK * MOE_BLOCK
    pends = jnp.cumsum(pcounts)
    pstarts = pends - pcounts
    dest = pstarts[se] + jnp.arange(m) - starts[se]
    n_blk = -(-m // MOE_BLOCK) + N_EXPERTS
    n_rows = n_blk * MOE_BLOCK
    buf_t = jnp.full((n_rows,), t, jnp.int32).at[dest].set(st)
    buf_w = jnp.zeros((n_rows,), jnp.float32).at[dest].set(sw)
    blk_e = jnp.minimum(jnp.searchsorted(pends, jnp.arange(n_blk) * MOE_BLOCK, side='right'),
                        N_EXPERTS - 1)
    x_pad = jnp.concatenate([xf, jnp.zeros((1, d), xf.dtype)], axis=0)

    def expert_block(acc, blk):
        tok, wt, e = blk
        xb = x_pad[tok]
        hb = jax.nn.silu(xb @ w1[e]) * (xb @ w3[e])
        yb = (hb @ w2[e]).astype(jnp.float32) * wt[:, None]
        return acc.at[tok].add(yb), None

    routed, _ = lax.scan(expert_block, jnp.zeros((t + 1, d), jnp.float32),
                         (buf_t.reshape(n_blk, MOE_BLOCK), buf_w.reshape(n_blk, MOE_BLOCK), blk_e))
    shared = (jax.nn.silu(xf @ ws1) * (xf @ ws3)) @ ws2
    return (routed[:t].astype(x.dtype) + shared).reshape(b, s, d)


def setup_inputs(seed: int = 0) -> dict:
    key = jax.random.key(seed)
    ks = jax.random.split(key, 25)
    L = DEPTH
    f32 = jnp.float32

    def nrm(k, shape, scale):
        return jax.random.normal(k, shape, f32) * scale

    return {
        'x': nrm(ks[0], (BATCH, SEQ, D_MODEL), 1.0),
        'mem': nrm(ks[1], (BATCH, MEM_LEN, D_MODEL), 1.0),
        'w_in': nrm(ks[2], (L, D_MODEL, IN_COLS), D_MODEL ** -0.5),
        'sgu_ln_g': 1.0 + nrm(ks[3], (L, N_SGU_GROUPS, SGU_GROUP_DIM), 0.01),
        'sgu_ln_b': nrm(ks[4], (L, N_SGU_GROUPS, SGU_GROUP_DIM), 0.01),
        'w_s': nrm(ks[5], (L, N_SGU_GROUPS, SGU_CHUNK, SGU_CHUNK), SGU_CHUNK ** -0.5),
        'b_s': 1.0 + nrm(ks[6], (L, N_SGU_GROUPS, SGU_CHUNK), 0.01),
        'w_out': nrm(ks[7], (L, MIX_WIDTH, D_MODEL), MIX_WIDTH ** -0.5 * DEEPNORM_BETA),
        'ln1_g': 1.0 + nrm(ks[8], (L, D_MODEL), 0.01),
        'ln1_b': nrm(ks[9], (L, D_MODEL), 0.01),
        'w_mq': nrm(ks[10], (L, D_MODEL, MEM_HEADS * MEM_HEAD_DIM), D_MODEL ** -0.5),
        'w_mkv': nrm(ks[11], (L, D_MODEL, 2 * MEM_HEADS * MEM_HEAD_DIM), D_MODEL ** -0.5),
        'w_mo': nrm(ks[12], (L, MEM_HEADS * MEM_HEAD_DIM, D_MODEL),
                    (MEM_HEADS * MEM_HEAD_DIM) ** -0.5 * DEEPNORM_BETA),
        'ln2_g': 1.0 + nrm(ks[13], (L, D_MODEL), 0.01),
        'ln2_b': nrm(ks[14], (L, D_MODEL), 0.01),
        'w_router': nrm(ks[15], (L, D_MODEL, N_EXPERTS), D_MODEL ** -0.5),
        'router_bias': nrm(ks[16], (L, N_EXPERTS), 0.01),
        'w1': nrm(ks[17], (L, N_EXPERTS, D_MODEL, EXPERT_DIM), D_MODEL ** -0.5),
        'w3': nrm(ks[18], (L, N_EXPERTS, D_MODEL, EXPERT_DIM), D_MODEL ** -0.5),
        'w2': nrm(ks[19], (L, N_EXPERTS, EXPERT_DIM, D_MODEL), EXPERT_DIM ** -0.5 * DEEPNORM_BETA),
        'ws1': nrm(ks[20], (L, D_MODEL, SHARED_DIM), D_MODEL ** -0.5),
        'ws3': nrm(ks[21], (L, D_MODEL, SHARED_DIM), D_MODEL ** -0.5),
        'ws2': nrm(ks[22], (L, SHARED_DIM, D_MODEL), SHARED_DIM ** -0.5 * DEEPNORM_BETA),
        'ln3_g': 1.0 + nrm(ks[23], (L, D_MODEL), 0.01),
        'ln3_b': nrm(ks[24], (L, D_MODEL), 0.01),
    }


def reference(x, mem, w_in, sgu_ln_g, sgu_ln_b, w_s, b_s, w_out, ln1_g, ln1_b,
              w_mq, w_mkv, w_mo, ln2_g, ln2_b, w_router, router_bias, w1, w3, w2,
              ws1, ws3, ws2, ln3_g, ln3_b):
    for l in range(DEPTH):
        mix = hybrid_mixer(x, w_in[l], sgu_ln_g[l], sgu_ln_b[l], w_s[l], b_s[l], w_out[l])
        x = layer_norm(DEEPNORM_ALPHA * x + mix, ln1_g[l], ln1_b[l])
        xat = memory_cross_attention(x, mem, w_mq[l], w_mkv[l], w_mo[l])
        x = layer_norm(DEEPNORM_ALPHA * x + xat, ln2_g[l], ln2_b[l])
        ffn = moe_ffn(x, w_router[l], router_bias[l], w1[l], w3[l], w2[l], ws1[l], ws3[l], ws2[l])
        x = layer_norm(DEEPNORM_ALPHA * x + ffn, ln3_g[l], ln3_b[l])
    return x
```

```python
import functools

import jax
import jax.numpy as jnp
from jax import lax
from jax.experimental import pallas as pl
from jax.experimental.pallas import tpu as pltpu

F32 = jnp.float32
BF16 = jnp.bfloat16

N_ATTN_HEADS = 8
HEAD_DIM = 128
ATTN_WIDTH = N_ATTN_HEADS * HEAD_DIM
N_SGU_GROUPS = 8
SGU_GROUP_DIM = 128
SGU_WIDTH = N_SGU_GROUPS * SGU_GROUP_DIM
SGU_CHUNK = 128
MOBA_BLOCK = 256
MOBA_TOPK = 3
ROPE_THETA = 500000.0
ROPE_DIM = HEAD_DIM // 4
MEM_HEADS = 4
MEM_HEAD_DIM = 128
N_EXPERTS = 256
TOP_K = 8
N_EXPERT_GROUPS = 8
GROUP_SIZE = N_EXPERTS // N_EXPERT_GROUPS
TOPK_GROUPS = 4
ROUTED_SCALE = 2.5
MOE_BLOCK = 128
LN_EPS = 1e-5
DEPTH = 1
DEEPNORM_ALPHA = (2 * DEPTH) ** 0.25

LANES = 128
MASK_NEG = -1e30
VMEM_LIMIT = 56 * 1024 * 1024

_NT = (((1,), (1,)), ((), ()))


def _params(n_axes, vmem=VMEM_LIMIT):
    return pltpu.CompilerParams(dimension_semantics=("arbitrary",) * n_axes,
                                vmem_limit_bytes=vmem)


def _layer_norm(x, g, b):
    mu = jnp.mean(x, axis=-1, keepdims=True)
    xc = x - mu
    var = jnp.mean(xc * xc, axis=-1, keepdims=True)
    return xc * lax.rsqrt(var + LN_EPS) * g + b


def _silu(x):
    return x * jax.nn.sigmoid(x)


def _rope(acc, cosf, sinf):
    half = ROPE_DIM // 2
    rows = acc.shape[0]
    lane = lax.broadcasted_iota(jnp.int32, (rows, LANES), 1)
    outs = []
    for c in range(acc.shape[1] // LANES):
        t = acc[:, c * LANES:(c + 1) * LANES]
        partner = jnp.where(lane < half, pltpu.roll(t, LANES - half, 1), pltpu.roll(t, half, 1))
        outs.append(t * cosf + partner * sinf)
    return jnp.concatenate(outs, axis=1)


def _q_kernel(x_ref, w_ref, cos_ref, sin_ref, qs_ref, qf_ref):
    acc = jnp.dot(x_ref[...].astype(BF16), w_ref[...], preferred_element_type=F32)
    r = _rope(acc, cos_ref[...], sin_ref[...])
    qf_ref[...] = r
    qs_ref[...] = (r * (HEAD_DIM ** -0.5)).astype(BF16)


def _k_kernel(x_ref, w_ref, cos_ref, sin_ref, k_ref, km_ref):
    acc = jnp.dot(x_ref[...].astype(BF16), w_ref[...], preferred_element_type=F32)
    r = _rope(acc, cos_ref[...], sin_ref[...])
    k_ref[...] = r.astype(BF16)
    rows, cols = r.shape
    km_ref[0] = jnp.mean(r.reshape(rows // MOBA_BLOCK, MOBA_BLOCK, cols), axis=1)


def _v_kernel(x_ref, w_ref, v_ref):
    v_ref[...] = jnp.dot(x_ref[...].astype(BF16), w_ref[...],
                         preferred_element_type=F32).astype(BF16)


def _sgu_kernel(x_ref, wu_ref, wz_ref, g_ref, b_ref, ws_ref, bs_ref, o_ref):
    xb = x_ref[...].astype(BF16)
    u = jax.nn.gelu(jnp.dot(xb, wu_ref[...], preferred_element_type=F32))
    z = jax.nn.gelu(jnp.dot(xb, wz_ref[...], preferred_element_type=F32))
    rows, cols = u.shape
    t_row = lax.broadcasted_iota(jnp.int32, (SGU_CHUNK, SGU_CHUNK), 0)
    s_col = lax.broadcasted_iota(jnp.int32, (SGU_CHUNK, SGU_CHUNK), 1)
    causal = s_col <= t_row
    for gl in range(cols // SGU_GROUP_DIM):
        sl = slice(gl * SGU_GROUP_DIM, (gl + 1) * SGU_GROUP_DIM)
        zn = _layer_norm(z[:, sl], g_ref[:, sl], b_ref[:, sl]).astype(BF16)
        w = jnp.where(causal, ws_ref[gl], 0.0).astype(BF16)
        bias = bs_ref[:, gl:gl + 1]
        for c in range(rows // SGU_CHUNK):
            rs = slice(c * SGU_CHUNK, (c + 1) * SGU_CHUNK)
            mixed = jnp.dot(w, zn[rs], preferred_element_type=F32) + bias
            o_ref[rs, sl] = (u[rs, sl] * mixed).astype(BF16)


def _projections(x2d, w_in_bf, sgu_ln_g, sgu_ln_b, w_s, b_s, tm):
    s, d = x2d.shape
    tn = 512
    nrow = s // tm
    half = ROPE_DIM // 2

    pos = jnp.arange(s)
    inv_freq = ROPE_THETA ** (-jnp.arange(half, dtype=F32) / half)
    ang = pos.astype(F32)[:, None] * inv_freq[None, :]
    cos, sin = jnp.cos(ang), jnp.sin(ang)
    rest = LANES - ROPE_DIM
    cosf = jnp.concatenate([cos, cos, jnp.ones((s, rest), F32)], axis=1)
    sinf = jnp.concatenate([-sin, sin, jnp.zeros((s, rest), F32)], axis=1)

    x_spec = pl.BlockSpec((tm, d), lambda i, j: (i, 0))
    tab_spec = pl.BlockSpec((tm, LANES), lambda i, j: (i, 0))
    out_spec = pl.BlockSpec((tm, tn), lambda i, j: (i, j))
    nq = ATTN_WIDTH // tn

    def w_spec(col0):
        return pl.BlockSpec((d, tn), lambda i, j: (0, col0 // tn + j))

    qs, qf = pl.pallas_call(
        _q_kernel, grid=(nrow, nq),
        in_specs=[x_spec, w_spec(0), tab_spec, tab_spec],
        out_specs=[out_spec, out_spec],
        out_shape=[jax.ShapeDtypeStruct((s, ATTN_WIDTH), BF16),
                   jax.ShapeDtypeStruct((s, ATTN_WIDTH), F32)],
        compiler_params=_params(2), name="proj_q",
    )(x2d, w_in_bf, cosf, sinf)

    kpb = tm // MOBA_BLOCK
    kb, kmean = pl.pallas_call(
        _k_kernel, grid=(nrow, nq),
        in_specs=[x_spec, w_spec(ATTN_WIDTH), tab_spec, tab_spec],
        out_specs=[out_spec, pl.BlockSpec((1, kpb, tn), lambda i, j: (i, 0, j))],
        out_shape=[jax.ShapeDtypeStruct((s, ATTN_WIDTH), BF16),
                   jax.ShapeDtypeStruct((nrow, kpb, ATTN_WIDTH), F32)],
        compiler_params=_params(2), name="proj_k",
    )(x2d, w_in_bf, cosf, sinf)
    kmean = kmean.reshape(s // MOBA_BLOCK, ATTN_WIDTH)

    vb = pl.pallas_call(
        _v_kernel, grid=(nrow, nq),
        in_specs=[x_spec, w_spec(2 * ATTN_WIDTH)],
        out_specs=out_spec,
        out_shape=jax.ShapeDtypeStruct((s, ATTN_WIDTH), BF16),
        compiler_params=_params(2), name="proj_v",
    )(x2d, w_in_bf)

    gpb = tn // SGU_GROUP_DIM
    sgu = pl.pallas_call(
        _sgu_kernel, grid=(nrow, SGU_WIDTH // tn),
        in_specs=[x_spec, w_spec(3 * ATTN_WIDTH), w_spec(3 * ATTN_WIDTH + SGU_WIDTH),
                  pl.BlockSpec((1, tn), lambda i, j: (0, j)),
                  pl.BlockSpec((1, tn), lambda i, j: (0, j)),
                  pl.BlockSpec((gpb, SGU_CHUNK, SGU_CHUNK), lambda i, j: (j, 0, 0)),
                  pl.BlockSpec((None, SGU_CHUNK, gpb), lambda i, j: (j, 0, 0))],
        out_specs=out_spec,
        out_shape=jax.ShapeDtypeStruct((s, SGU_WIDTH), BF16),
        compiler_params=_params(2), name="proj_sgu",
    )(x2d, w_in_bf, w_in_bf, sgu_ln_g.reshape(1, SGU_WIDTH), sgu_ln_b.reshape(1, SGU_WIDTH),
      w_s, b_s.reshape(N_SGU_GROUPS // gpb, gpb, SGU_CHUNK).transpose(0, 2, 1))
    return qs, qf, kb, kmean, vb, sgu


def _moba_kernel(qs_ref, qf_ref, km_ref, k_ref, v_ref, o_ref, m_scr, l_scr, acc_scr, *, tq):
    i = pl.program_id(1)
    blk = MOBA_BLOCK
    own = (i * tq) // blk
    qs = qs_ref[...]

    gate = lax.dot_general(qf_ref[...], km_ref[...], _NT, precision=lax.Precision.HIGHEST,
                           preferred_element_type=F32)
    col = lax.broadcasted_iota(jnp.int32, (tq, LANES), 1)
    colf = col.astype(F32)
    gate = jnp.where(col < own, gate, -jnp.inf)
    selbias = jnp.full((tq, LANES), MASK_NEG, F32)
    for _ in range(MOBA_TOPK):
        gmax = jnp.max(gate, axis=-1, keepdims=True)
        cand = jnp.where(gate == gmax, colf, float(LANES))
        cand = jnp.where(gmax > -jnp.inf, cand, float(LANES))
        first = jnp.min(cand, axis=-1, keepdims=True)
        pick = colf == first
        selbias = jnp.where(pick, 0.0, selbias)
        gate = jnp.where(pick, -jnp.inf, gate)
    qaug = jnp.concatenate([qs, selbias.astype(BF16)], axis=1)

    own0 = pl.multiple_of(own * blk, blk)
    s = lax.dot_general(qs, k_ref[pl.ds(own0, blk), :], _NT, preferred_element_type=F32)
    q_pos = i * tq + lax.broadcasted_iota(jnp.int32, (tq, blk), 0)
    k_pos = own * blk + lax.broadcasted_iota(jnp.int32, (tq, blk), 1)
    s = jnp.where(k_pos <= q_pos, s, MASK_NEG)
    m0 = jnp.max(s, axis=-1, keepdims=True)
    p = jnp.exp(s - m0)
    m_scr[...] = jnp.broadcast_to(m0, (tq, LANES))
    l_scr[...] = jnp.broadcast_to(jnp.sum(p, axis=-1, keepdims=True), (tq, LANES))
    acc_scr[...] = jnp.dot(p.astype(BF16), v_ref[pl.ds(own0, blk), :],
                           preferred_element_type=F32)

    blk_lane = lax.broadcasted_iota(jnp.int32, (blk, LANES), 1)

    def body(n, carry):
        n0 = pl.multiple_of(n * blk, blk)
        onehot = jnp.where(blk_lane == n, 1.0, 0.0).astype(BF16)
        kaug = jnp.concatenate([k_ref[pl.ds(n0, blk), :], onehot], axis=1)
        sc = lax.dot_general(qaug, kaug, _NT, preferred_element_type=F32)
        m_old = m_scr[...]
        m_new = jnp.maximum(m_old, jnp.max(sc, axis=-1, keepdims=True))
        alpha = jnp.exp(m_old - m_new)
        pn = jnp.exp(sc - m_new[:, :1])
        l_scr[...] = alpha * l_scr[...] + jnp.sum(pn, axis=-1, keepdims=True)
        acc_scr[...] = alpha * acc_scr[...] + jnp.dot(
            pn.astype(BF16), v_ref[pl.ds(n0, blk), :], preferred_element_type=F32)
        m_scr[...] = m_new
        return carry

    lax.fori_loop(0, own, body, 0)
    o_ref[...] = (acc_scr[...] / l_scr[...]).astype(BF16)


def _moba(qs, qf, kb, kmean, vb, tq):
    s = qs.shape[0]
    nb = s // MOBA_BLOCK
    assert nb <= LANES and tq in (128, 256) and s % tq == 0
    km_pad = jnp.zeros((LANES, ATTN_WIDTH), F32).at[:nb].set(kmean)
    tile = pl.BlockSpec((tq, HEAD_DIM), lambda h, i: (i, h))
    return pl.pallas_call(
        functools.partial(_moba_kernel, tq=tq),
        grid=(N_ATTN_HEADS, s // tq),
        in_specs=[tile, tile,
                  pl.BlockSpec((LANES, HEAD_DIM), lambda h, i: (0, h)),
                  pl.BlockSpec((s, HEAD_DIM), lambda h, i: (0, h)),
                  pl.BlockSpec((s, HEAD_DIM), lambda h, i: (0, h))],
        out_specs=tile,
        out_shape=jax.ShapeDtypeStruct((s, ATTN_WIDTH), BF16),
        scratch_shapes=[pltpu.VMEM((tq, LANES), F32)] * 3,
        compiler_params=_params(2), name="moba",
    )(qs, qf, km_pad, kb, vb)


def _memkv_kernel(mem_ref, w_ref, kv_ref):
    kv_ref[...] = jnp.dot(mem_ref[...].astype(BF16), w_ref[...],
                          preferred_element_type=F32).astype(BF16)


def _postmix_kernel(attn_ref, sgu_ref, x_ref, wout_ref, g1_ref, b1_ref, wmq_ref, kv_ref,
                    wmo_ref, g2_ref, b2_ref, o_ref):
    mixin = jnp.concatenate([attn_ref[...], sgu_ref[...]], axis=1)
    mix = jnp.dot(mixin, wout_ref[...], preferred_element_type=F32)
    y1 = _layer_norm(DEEPNORM_ALPHA * x_ref[...] + mix, g1_ref[...], b1_ref[...])

    q = jnp.dot(y1.astype(BF16), wmq_ref[...], preferred_element_type=F32)
    q = (q * (MEM_HEAD_DIM ** -0.5)).astype(BF16)
    width = MEM_HEADS * MEM_HEAD_DIM
    heads = []
    for h in range(MEM_HEADS):
        sl = slice(h * MEM_HEAD_DIM, (h + 1) * MEM_HEAD_DIM)
        sc = lax.dot_general(q[:, sl], kv_ref[:, sl], _NT, preferred_element_type=F32)
        p = jnp.exp(sc - jnp.max(sc, axis=-1, keepdims=True))
        denom = jnp.sum(p, axis=-1, keepdims=True)
        vsl = slice(width + h * MEM_HEAD_DIM, width + (h + 1) * MEM_HEAD_DIM)
        heads.append(jnp.dot(p.astype(BF16), kv_ref[:, vsl], preferred_element_type=F32) / denom)
    o = jnp.concatenate(heads, axis=1).astype(BF16)
    xat = jnp.dot(o, wmo_ref[...], preferred_element_type=F32)
    o_ref[...] = _layer_norm(DEEPNORM_ALPHA * y1 + xat, g2_ref[...], b2_ref[...])


def _postmix(attn, sgu, x2d, mem2d, w_out, ln1_g, ln1_b, w_mq, w_mkv, w_mo, ln2_g, ln2_b, tm):
    s, d = x2d.shape
    m = mem2d.shape[0]
    width = MEM_HEADS * MEM_HEAD_DIM
    kv = pl.pallas_call(
        _memkv_kernel,
        out_shape=jax.ShapeDtypeStruct((m, 2 * width), BF16),
        compiler_params=_params(0), name="mem_kv",
    )(mem2d, w_mkv.astype(BF16))

    def full(shape):
        return pl.BlockSpec(shape, lambda i: (0,) * len(shape))

    def rows(cols):
        return pl.BlockSpec((tm, cols), lambda i: (i, 0))

    return pl.pallas_call(
        _postmix_kernel, grid=(s // tm,),
        in_specs=[rows(ATTN_WIDTH), rows(SGU_WIDTH), rows(d), full((ATTN_WIDTH + SGU_WIDTH, d)),
                  full((1, d)), full((1, d)), full((d, width)), full((m, 2 * width)),
                  full((width, d)), full((1, d)), full((1, d))],
        out_specs=rows(d),
        out_shape=jax.ShapeDtypeStruct((s, d), F32),
        compiler_params=_params(1), name="postmix",
    )(attn, sgu, x2d, w_out.astype(BF16), ln1_g.reshape(1, d), ln1_b.reshape(1, d),
      w_mq.astype(BF16), kv, w_mo.astype(BF16), ln2_g.reshape(1, d), ln2_b.reshape(1, d))


def _router_kernel(x_ref, wh_ref, wl_ref, bias_ref, eidx_ref, gw_ref):
    tm = x_ref.shape[0]
    x = x_ref[...]
    xh = x.astype(BF16)
    xl = (x - xh.astype(F32)).astype(BF16)
    wh = wh_ref[...]
    logits = (lax.dot_general(wh, xh, _NT, preferred_element_type=F32)
              + lax.dot_general(wh, xl, _NT, preferred_element_type=F32)
              + lax.dot_general(wl_ref[...], xh, _NT, preferred_element_type=F32))
    scores = jax.nn.sigmoid(logits)
    biased = scores + bias_ref[...]

    g3 = biased.reshape(N_EXPERT_GROUPS, GROUP_SIZE, tm)
    in_grp = lax.broadcasted_iota(jnp.int32, g3.shape, 1).astype(F32)
    top1 = jnp.max(g3, axis=1, keepdims=True)
    first = jnp.min(jnp.where(g3 == top1, in_grp, float(GROUP_SIZE)), axis=1, keepdims=True)
    top2 = jnp.max(jnp.where(in_grp == first, -jnp.inf, g3), axis=1, keepdims=True)
    grp_score = (top1 + top2).reshape(N_EXPERT_GROUPS, tm)

    gid = lax.broadcasted_iota(jnp.int32, grp_score.shape, 0).astype(F32)
    grp_keep = jnp.zeros(grp_score.shape, F32)
    for _ in range(TOPK_GROUPS):
        best = jnp.max(grp_score, axis=0, keepdims=True)
        first = jnp.min(jnp.where(grp_score == best, gid, float(N_EXPERT_GROUPS)),
                        axis=0, keepdims=True)
        pick = gid == first
        grp_keep = jnp.where(pick, 1.0, grp_keep)
        grp_score = jnp.where(pick, -jnp.inf, grp_score)
    keep = jnp.broadcast_to(grp_keep.reshape(N_EXPERT_GROUPS, 1, tm), g3.shape) > 0.5
    masked = jnp.where(keep, g3, -jnp.inf).reshape(N_EXPERTS, tm)

    eid = lax.broadcasted_iota(jnp.int32, masked.shape, 0).astype(F32)
    idx_rows, w_rows = [], []
    for _ in range(TOP_K):
        best = jnp.max(masked, axis=0, keepdims=True)
        first = jnp.min(jnp.where(masked == best, eid, float(N_EXPERTS)), axis=0, keepdims=True)
        pick = eid == first
        idx_rows.append(first)
        w_rows.append(jnp.sum(jnp.where(pick, scores, 0.0), axis=0, keepdims=True))
        masked = jnp.where(pick, -jnp.inf, masked)
    gw = jnp.concatenate(w_rows, axis=0)
    gw = gw / jnp.sum(gw, axis=0, keepdims=True) * ROUTED_SCALE
    eidx_ref[...] = jnp.concatenate(idx_rows, axis=0).astype(jnp.int32)
    gw_ref[...] = gw


def _router(x2, w_router, router_bias, tm):
    t, d = x2.shape
    wr_t = w_router.T.astype(F32)
    wr_hi = wr_t.astype(BF16)
    wr_lo = (wr_t - wr_hi.astype(F32)).astype(BF16)
    return pl.pallas_call(
        _router_kernel, grid=(t // tm,),
        in_specs=[pl.BlockSpec((tm, d), lambda i: (i, 0)),
                  pl.BlockSpec((N_EXPERTS, d), lambda i: (0, 0)),
                  pl.BlockSpec((N_EXPERTS, d), lambda i: (0, 0)),
                  pl.BlockSpec((N_EXPERTS, 1), lambda i: (0, 0))],
        out_specs=[pl.BlockSpec((TOP_K, tm), lambda i: (0, i)),
                   pl.BlockSpec((TOP_K, tm), lambda i: (0, i))],
        out_shape=[jax.ShapeDtypeStruct((TOP_K, t), jnp.int32),
                   jax.ShapeDtypeStruct((TOP_K, t), F32)],
        compiler_params=_params(1), name="router",
    )(x2, wr_hi, wr_lo, router_bias.reshape(N_EXPERTS, 1).astype(F32))


def _expert_kernel(blk_e_ref, nused_ref, ids_hbm, x_hbm, w1_ref, w3_ref, w2_ref, y_hbm,
                   ids_smem, xbuf, ybuf, isem, gsem, ssem):
    b = pl.program_id(0)
    nused = nused_ref[0]
    rows = MOE_BLOCK

    def ids_copy(blk, slot):
        return pltpu.make_async_copy(ids_hbm.at[blk], ids_smem.at[slot], isem.at[slot])

    def start_gather(islot, slot):
        def body(r, c):
            tok = ids_smem[islot, r]
            pltpu.make_async_copy(x_hbm.at[pl.ds(tok, 1)], xbuf.at[slot, pl.ds(r, 1)],
                                  gsem.at[slot]).start()
            return c
        lax.fori_loop(0, rows, body, 0, unroll=8)

    def wait_gather(slot):
        pltpu.make_async_copy(x_hbm.at[pl.ds(0, rows)], xbuf.at[slot], gsem.at[slot]).wait()

    def start_scatter(islot, slot):
        def body(r, c):
            dst = ids_smem[islot, rows + r]
            pltpu.make_async_copy(ybuf.at[slot, pl.ds(r, 1)], y_hbm.at[pl.ds(dst, 1)],
                                  ssem.at[slot]).start()
            return c
        lax.fori_loop(0, rows, body, 0, unroll=8)

    def wait_scatter(slot):
        pltpu.make_async_copy(ybuf.at[slot], y_hbm.at[pl.ds(0, rows)], ssem.at[slot]).wait()

    @pl.when(b == 0)
    def _():
        ids_copy(0, 0).start()

        @pl.when(nused > 1)
        def _():
            ids_copy(1, 1).start()

        n_valid = y_hbm.shape[0] - 2 * rows
        ybuf[0] = jnp.zeros(ybuf.shape[1:], F32)
        for half in range(2):
            spare = pltpu.make_async_copy(ybuf.at[0], y_hbm.at[pl.ds(n_valid + half * rows, rows)],
                                          ssem.at[half])
            spare.start()
            spare.wait()

        ids_copy(0, 0).wait()
        start_gather(0, 0)

    @pl.when(b < nused)
    def _():
        slot = b % 2
        islot = b % 3

        @pl.when(b + 2 < nused)
        def _():
            ids_copy(b + 2, (b + 2) % 3).start()

        @pl.when(b + 1 < nused)
        def _():
            ids_copy(b + 1, (b + 1) % 3).wait()
            start_gather((b + 1) % 3, (b + 1) % 2)

        wait_gather(slot)

        @pl.when(b >= 2)
        def _():
            wait_scatter(slot)

        xb = xbuf[slot]
        h = _silu(jnp.dot(xb, w1_ref[...], preferred_element_type=F32)) * jnp.dot(
            xb, w3_ref[...], preferred_element_type=F32)
        ybuf[slot] = jnp.dot(h, w2_ref[...], preferred_element_type=F32)
        start_scatter(islot, slot)

        @pl.when(b == nused - 1)
        def _():
            wait_scatter(slot)

            @pl.when(b >= 1)
            def _():
                wait_scatter(1 - slot)


def _routed_experts(x2, eidx, w1, w3, w2):
    t, d = x2.shape
    e, _, f = w1.shape
    m = t * TOP_K
    flat_e = eidx.reshape(m)
    order = jnp.argsort(flat_e).astype(jnp.int32)
    se = flat_e[order]
    counts = jnp.bincount(flat_e, length=e)
    starts = jnp.cumsum(counts) - counts
    pcounts = (counts + MOE_BLOCK - 1) // MOE_BLOCK * MOE_BLOCK
    pends = jnp.cumsum(pcounts)
    pstarts = pends - pcounts
    dest = pstarts[se] + jnp.arange(m) - starts[se]
    n_blk = -(-m // MOE_BLOCK) + e
    n_rows = n_blk * MOE_BLOCK
    buf_a = jnp.full((n_rows,), m, jnp.int32).at[dest].set(order)
    nused = (pends[-1] // MOE_BLOCK).astype(jnp.int32)
    blk_e = jnp.minimum(jnp.searchsorted(pends, jnp.arange(n_blk) * MOE_BLOCK, side='right'), e - 1)
    last_e = blk_e[jnp.maximum(nused - 1, 0)]
    blk_e = jnp.where(jnp.arange(n_blk) < nused, blk_e, last_e).astype(jnp.int32)
    valid = buf_a < m
    gtok = jnp.where(valid, buf_a // TOP_K, 0)
    srow = jnp.where(valid, (buf_a % TOP_K) * t + buf_a // TOP_K,
                     m + jnp.arange(n_rows, dtype=jnp.int32) % (2 * MOE_BLOCK))
    ids = jnp.concatenate([gtok.reshape(n_blk, MOE_BLOCK), srow.reshape(n_blk, MOE_BLOCK)],
                          axis=1).astype(jnp.int32)

    grid_spec = pltpu.PrefetchScalarGridSpec(
        num_scalar_prefetch=2, grid=(n_blk,),
        in_specs=[pl.BlockSpec(memory_space=pl.ANY),
                  pl.BlockSpec(memory_space=pl.ANY),
                  pl.BlockSpec((None, d, f), lambda b, be, nu: (be[b], 0, 0)),
                  pl.BlockSpec((None, d, f), lambda b, be, nu: (be[b], 0, 0)),
                  pl.BlockSpec((None, f, d), lambda b, be, nu: (be[b], 0, 0))],
        out_specs=pl.BlockSpec(memory_space=pl.ANY),
        scratch_shapes=[pltpu.SMEM((3, 2 * MOE_BLOCK), jnp.int32),
                        pltpu.VMEM((2, MOE_BLOCK, d), F32),
                        pltpu.VMEM((2, MOE_BLOCK, d), F32),
                        pltpu.SemaphoreType.DMA((3,)),
                        pltpu.SemaphoreType.DMA((2,)),
                        pltpu.SemaphoreType.DMA((2,))])
    return pl.pallas_call(
        _expert_kernel, grid_spec=grid_spec,
        out_shape=jax.ShapeDtypeStruct((m + 2 * MOE_BLOCK, d), F32),
        compiler_params=_params(1), name="experts",
    )(blk_e, nused.reshape(1), ids, x2, w1, w3, w2)


def _final_kernel(x_ref, gw_ref, *refs):
    y_refs = refs[:TOP_K]
    ws1_ref, ws3_ref, ws2_ref, g_ref, b_ref, o_ref = refs[TOP_K:]
    x = x_ref[...]
    xb = x.astype(BF16)
    hid = _silu(jnp.dot(xb, ws1_ref[...], preferred_element_type=F32)) * jnp.dot(
        xb, ws3_ref[...], preferred_element_type=F32)
    shared = jnp.dot(hid.astype(BF16), ws2_ref[...], preferred_element_type=F32)
    gw = gw_ref[...]
    routed = gw[:, 0:1] * y_refs[0][...]
    for k in range(1, TOP_K):
        routed = routed + gw[:, k:k + 1] * y_refs[k][...]
    o_ref[...] = _layer_norm(DEEPNORM_ALPHA * x + (routed + shared), g_ref[...], b_ref[...])


def _final(x2, gw, y, ws1, ws3, ws2, ln3_g, ln3_b, tm):
    t, d = x2.shape
    f = ws1.shape[1]
    nrow = t // tm

    def full(shape):
        return pl.BlockSpec(shape, lambda i: (0,) * len(shape))

    y_specs = [pl.BlockSpec((tm, d), functools.partial(lambda i, k: (k * nrow + i, 0), k=k))
               for k in range(TOP_K)]
    return pl.pallas_call(
        _final_kernel, grid=(nrow,),
        in_specs=[pl.BlockSpec((tm, d), lambda i: (i, 0)),
                  pl.BlockSpec((tm, TOP_K), lambda i: (i, 0))] + y_specs +
                 [full((d, f)), full((d, f)), full((f, d)), full((1, d)), full((1, d))],
        out_specs=pl.BlockSpec((tm, d), lambda i: (i, 0)),
        out_shape=jax.ShapeDtypeStruct((t, d), F32),
        compiler_params=_params(1), name="final",
    )(x2, gw, *([y] * TOP_K), ws1.astype(BF16), ws3.astype(BF16), ws2.astype(BF16),
      ln3_g.reshape(1, d), ln3_b.reshape(1, d))


def kernel(x, mem, w_in, sgu_ln_g, sgu_ln_b, w_s, b_s, w_out, ln1_g, ln1_b, w_mq, w_mkv, w_mo,
           ln2_g, ln2_b, w_router, router_bias, w1, w3, w2, ws1, ws3, ws2, ln3_g, ln3_b):
    b, s, d = x.shape
    assert b == 1
    h = x.reshape(s, d)
    mem2d = mem.reshape(mem.shape[1], d)
    for l in range(DEPTH):
        qs, qf, kb, kmean, vb, sgu = _projections(
            h, w_in[l].astype(BF16), sgu_ln_g[l], sgu_ln_b[l], w_s[l], b_s[l], tm=min(512, s))
        attn = _moba(qs, qf, kb, kmean, vb, tq=256)
        x2 = _postmix(attn, sgu, h, mem2d, w_out[l], ln1_g[l], ln1_b[l], w_mq[l], w_mkv[l],
                      w_mo[l], ln2_g[l], ln2_b[l], tm=256)
        eidx_t, gw_t = _router(x2, w_router[l], router_bias[l], tm=min(512, s))
        y = _routed_experts(x2, eidx_t.T, w1[l], w3[l], w2[l])
        h = _final(x2, gw_t.T, y, ws1[l], ws3[l], ws2[l], ln3_g[l], ln3_b[l], tm=128)
    return h.reshape(b, s, d)
```

```python
import functools

import jax
import jax.numpy as jnp
from jax import lax
from jax.experimental import pallas as pl
from jax.experimental.pallas import tpu as pltpu

F32 = jnp.float32
BF16 = jnp.bfloat16

N_ATTN_HEADS = 8
HEAD_DIM = 128
ATTN_WIDTH = N_ATTN_HEADS * HEAD_DIM
N_SGU_GROUPS = 8
SGU_GROUP_DIM = 128
SGU_WIDTH = N_SGU_GROUPS * SGU_GROUP_DIM
SGU_CHUNK = 128
MOBA_BLOCK = 256
MOBA_TOPK = 3
ROPE_THETA = 500000.0
ROPE_DIM = HEAD_DIM // 4
MEM_HEADS = 4
MEM_HEAD_DIM = 128
N_EXPERTS = 256
TOP_K = 8
N_EXPERT_GROUPS = 8
GROUP_SIZE = N_EXPERTS // N_EXPERT_GROUPS
TOPK_GROUPS = 4
ROUTED_SCALE = 2.5
MOE_BLOCK = 128
LN_EPS = 1e-5
DEPTH = 1
DEEPNORM_ALPHA = (2 * DEPTH) ** 0.25

LANES = 128
MASK_NEG = -1e30
LOG2_E = 1.4426950408889634
VMEM_LIMIT = 56 * 1024 * 1024

_NT = (((1,), (1,)), ((), ()))


def _params(n_axes, vmem=VMEM_LIMIT):
    return pltpu.CompilerParams(dimension_semantics=("arbitrary",) * n_axes,
                                vmem_limit_bytes=vmem)


def _layer_norm(x, g, b):
    mu = jnp.mean(x, axis=-1, keepdims=True)
    xc = x - mu
    var = jnp.mean(xc * xc, axis=-1, keepdims=True)
    return xc * lax.rsqrt(var + LN_EPS) * g + b


def _silu(x):
    return x * jax.nn.sigmoid(x)


def _rope(acc, cosf, sinf):
    half = ROPE_DIM // 2
    rows = acc.shape[0]
    lane = lax.broadcasted_iota(jnp.int32, (rows, LANES), 1)
    outs = []
    for c in range(acc.shape[1] // LANES):
        t = acc[:, c * LANES:(c + 1) * LANES]
        partner = jnp.where(lane < half, pltpu.roll(t, LANES - half, 1), pltpu.roll(t, half, 1))
        outs.append(t * cosf + partner * sinf)
    return jnp.concatenate(outs, axis=1)


def _q_kernel(x_ref, w_ref, cos_ref, sin_ref, qs_ref, qf_ref):
    acc = jnp.dot(x_ref[...].astype(BF16), w_ref[...], preferred_element_type=F32)
    r = _rope(acc, cos_ref[...], sin_ref[...])
    qf_ref[...] = r
    qs_ref[...] = (r * (HEAD_DIM ** -0.5 * LOG2_E)).astype(BF16)


def _k_kernel(x_ref, w_ref, cos_ref, sin_ref, k_ref, km_ref):
    acc = jnp.dot(x_ref[...].astype(BF16), w_ref[...], preferred_element_type=F32)
    r = _rope(acc, cos_ref[...], sin_ref[...])
    k_ref[...] = r.astype(BF16)
    rows, cols = r.shape
    km_ref[0] = jnp.mean(r.reshape(rows // MOBA_BLOCK, MOBA_BLOCK, cols), axis=1)


def _v_kernel(x_ref, w_ref, v_ref):
    v_ref[...] = jnp.dot(x_ref[...].astype(BF16), w_ref[...],
                         preferred_element_type=F32).astype(BF16)


def _sgu_kernel(x_ref, wu_ref, wz_ref, g_ref, b_ref, ws_ref, bs_ref, o_ref):
    xb = x_ref[...].astype(BF16)
    u = jax.nn.gelu(jnp.dot(xb, wu_ref[...], preferred_element_type=F32))
    z = jax.nn.gelu(jnp.dot(xb, wz_ref[...], preferred_element_type=F32))
    rows, cols = u.shape
    t_row = lax.broadcasted_iota(jnp.int32, (SGU_CHUNK, SGU_CHUNK), 0)
    s_col = lax.broadcasted_iota(jnp.int32, (SGU_CHUNK, SGU_CHUNK), 1)
    causal = s_col <= t_row
    for gl in range(cols // SGU_GROUP_DIM):
        sl = slice(gl * SGU_GROUP_DIM, (gl + 1) * SGU_GROUP_DIM)
        zn = _layer_norm(z[:, sl], g_ref[:, sl], b_ref[:, sl]).astype(BF16)
        w = jnp.where(causal, ws_ref[gl], 0.0).astype(BF16)
        bias = bs_ref[:, gl:gl + 1]
        for c in range(rows // SGU_CHUNK):
            rs = slice(c * SGU_CHUNK, (c + 1) * SGU_CHUNK)
            mixed = jnp.dot(w, zn[rs], preferred_element_type=F32) + bias
            o_ref[rs, sl] = (u[rs, sl] * mixed).astype(BF16)


def _projections(x2d, w_in_bf, sgu_ln_g, sgu_ln_b, w_s, b_s, tm):
    s, d = x2d.shape
    tn = 512
    nrow = s // tm
    half = ROPE_DIM // 2

    pos = jnp.arange(s)
    inv_freq = ROPE_THETA ** (-jnp.arange(half, dtype=F32) / half)
    ang = pos.astype(F32)[:, None] * inv_freq[None, :]
    cos, sin = jnp.cos(ang), jnp.sin(ang)
    rest = LANES - ROPE_DIM
    cosf = jnp.concatenate([cos, cos, jnp.ones((s, rest), F32)], axis=1)
    sinf = jnp.concatenate([-sin, sin, jnp.zeros((s, rest), F32)], axis=1)

    x_spec = pl.BlockSpec((tm, d), lambda i, j: (i, 0))
    tab_spec = pl.BlockSpec((tm, LANES), lambda i, j: (i, 0))
    out_spec = pl.BlockSpec((tm, tn), lambda i, j: (i, j))
    nq = ATTN_WIDTH // tn

    def w_spec(col0):
        return pl.BlockSpec((d, tn), lambda i, j: (0, col0 // tn + j))

    qs, qf = pl.pallas_call(
        _q_kernel, grid=(nrow, nq),
        in_specs=[x_spec, w_spec(0), tab_spec, tab_spec],
        out_specs=[out_spec, out_spec],
        out_shape=[jax.ShapeDtypeStruct((s, ATTN_WIDTH), BF16),
                   jax.ShapeDtypeStruct((s, ATTN_WIDTH), F32)],
        compiler_params=_params(2), name="proj_q",
    )(x2d, w_in_bf, cosf, sinf)

    kpb = tm // MOBA_BLOCK
    kb, kmean = pl.pallas_call(
        _k_kernel, grid=(nrow, nq),
        in_specs=[x_spec, w_spec(ATTN_WIDTH), tab_spec, tab_spec],
        out_specs=[out_spec, pl.BlockSpec((1, kpb, tn), lambda i, j: (i, 0, j))],
        out_shape=[jax.ShapeDtypeStruct((s, ATTN_WIDTH), BF16),
                   jax.ShapeDtypeStruct((nrow, kpb, ATTN_WIDTH), F32)],
        compiler_params=_params(2), name="proj_k",
    )(x2d, w_in_bf, cosf, sinf)
    kmean = kmean.reshape(s // MOBA_BLOCK, ATTN_WIDTH)

    vb = pl.pallas_call(
        _v_kernel, grid=(nrow, nq),
        in_specs=[x_spec, w_spec(2 * ATTN_WIDTH)],
        out_specs=out_spec,
        out_shape=jax.ShapeDtypeStruct((s, ATTN_WIDTH), BF16),
        compiler_params=_params(2), name="proj_v",
    )(x2d, w_in_bf)

    gpb = tn // SGU_GROUP_DIM
    sgu = pl.pallas_call(
        _sgu_kernel, grid=(nrow, SGU_WIDTH // tn),
        in_specs=[x_spec, w_spec(3 * ATTN_WIDTH), w_spec(3 * ATTN_WIDTH + SGU_WIDTH),
                  pl.BlockSpec((1, tn), lambda i, j: (0, j)),
                  pl.BlockSpec((1, tn), lambda i, j: (0, j)),
                  pl.BlockSpec((gpb, SGU_CHUNK, SGU_CHUNK), lambda i, j: (j, 0, 0)),
                  pl.BlockSpec((None, SGU_CHUNK, gpb), lambda i, j: (j, 0, 0))],
        out_specs=out_spec,
        out_shape=jax.ShapeDtypeStruct((s, SGU_WIDTH), BF16),
        compiler_params=_params(2), name="proj_sgu",
    )(x2d, w_in_bf, w_in_bf, sgu_ln_g.reshape(1, SGU_WIDTH), sgu_ln_b.reshape(1, SGU_WIDTH),
      w_s, b_s.reshape(N_SGU_GROUPS // gpb, gpb, SGU_CHUNK).transpose(0, 2, 1))
    return qs, qf, kb, kmean, vb, sgu


def _moba_kernel(qs_ref, qf_ref, km_ref, oh_ref, k_ref, v_ref, o_ref,
                 qaug_scr, sa_scr, sb_scr, m_scr, l_scr, acc_scr, *, tq, ck, hs):
    i = pl.program_id(1)
    own = i
    n_full = own // (ck // MOBA_BLOCK)
    col = lax.broadcasted_iota(jnp.int32, (tq, LANES), 1)
    colf = col.astype(F32)

    for hh in range(hs):
        hsl = slice(hh * HEAD_DIM, (hh + 1) * HEAD_DIM)
        gate = lax.dot_general(qf_ref[:, hsl], km_ref[:, hsl], _NT,
                               precision=lax.Precision.HIGHEST,
                               preferred_element_type=F32)
        gate = jnp.where(col < own, gate, -jnp.inf)
        selbias = jnp.where(col == own, 0.0, MASK_NEG)
        for _ in range(MOBA_TOPK):
            gmax = jnp.max(gate, axis=-1, keepdims=True)
            cand = jnp.where(gate == gmax, colf, float(LANES))
            cand = jnp.where(gmax > -jnp.inf, cand, float(LANES))
            first = jnp.min(cand, axis=-1, keepdims=True)
            pick = colf == first
            selbias = jnp.where(pick, 0.0, selbias)
            gate = jnp.where(pick, -jnp.inf, gate)
        qaug_scr[hh] = jnp.concatenate([qs_ref[:, hsl], selbias.astype(BF16)], axis=1)

    def scores(c0, s_ref, causal=None):
        c0 = pl.multiple_of(c0, ck)
        onehot = oh_ref[pl.ds(c0, ck), :]
        for hh in range(hs):
            hsl = slice(hh * HEAD_DIM, (hh + 1) * HEAD_DIM)
            kaug = jnp.concatenate([k_ref[pl.ds(c0, ck), hsl], onehot], axis=1)
            sc = lax.dot_general(qaug_scr[hh], kaug, _NT, preferred_element_type=F32)
            if causal is not None:
                sc = jnp.where(causal, sc, MASK_NEG)
            s_ref[hh] = sc

    def absorb(c0, s_ref):
        c0 = pl.multiple_of(c0, ck)
        for hh in range(hs):
            hsl = slice(hh * HEAD_DIM, (hh + 1) * HEAD_DIM)
            sc = s_ref[hh]
            m_old = m_scr[hh]
            m_new = jnp.maximum(m_old, jnp.max(sc, axis=-1, keepdims=True))
            alpha = jnp.exp2(m_old - m_new)
            pn = jnp.exp2(sc - m_new[:, :1])
            l_scr[hh] = alpha * l_scr[hh] + jnp.sum(pn, axis=-1, keepdims=True)
            acc_scr[hh] = alpha * acc_scr[hh] + jnp.dot(
                pn.astype(BF16), v_ref[pl.ds(c0, ck), hsl], preferred_element_type=F32)
            m_scr[hh] = m_new

    m_scr[...] = jnp.full(m_scr.shape, MASK_NEG, F32)
    l_scr[...] = jnp.zeros(l_scr.shape, F32)
    acc_scr[...] = jnp.zeros(acc_scr.shape, F32)
    d0 = n_full * ck
    q_pos = i * tq + lax.broadcasted_iota(jnp.int32, (tq, ck), 0)
    k_pos = d0 + lax.broadcasted_iota(jnp.int32, (tq, ck), 1)
    scores(d0, sa_scr, k_pos <= q_pos)

    def pair(u, carry):
        prev0 = jnp.where(u == 0, d0, (2 * u - 1) * ck)
        absorb(prev0, sa_scr)
        scores(2 * u * ck, sb_scr)
        absorb(2 * u * ck, sb_scr)
        scores((2 * u + 1) * ck, sa_scr)
        return carry

    n_pairs = n_full // 2
    lax.fori_loop(0, n_pairs, pair, 0)
    last0 = jnp.where(n_pairs == 0, d0, (2 * n_pairs - 1) * ck)

    @pl.when(n_full % 2 == 1)
    def _():
        absorb(last0, sa_scr)
        scores((n_full - 1) * ck, sb_scr)
        absorb((n_full - 1) * ck, sb_scr)

    @pl.when(n_full % 2 == 0)
    def _():
        absorb(last0, sa_scr)

    for hh in range(hs):
        o_ref[:, hh * HEAD_DIM:(hh + 1) * HEAD_DIM] = (acc_scr[hh] / l_scr[hh]).astype(BF16)


def _moba(qs, qf, kb, kmean, vb, ck, hs):
    s = qs.shape[0]
    tq = MOBA_BLOCK
    nb = s // MOBA_BLOCK
    assert nb <= LANES and s % ck == 0 and ck % MOBA_BLOCK == 0 and N_ATTN_HEADS % hs == 0
    km_pad = jnp.zeros((LANES, ATTN_WIDTH), F32).at[:nb].set(kmean)
    onehot = (jnp.arange(s)[:, None] // MOBA_BLOCK == jnp.arange(LANES)[None, :]).astype(BF16)
    gw = hs * HEAD_DIM
    tile = pl.BlockSpec((tq, gw), lambda g, i: (i, g))
    once = pl.Buffered(1)
    return pl.pallas_call(
        functools.partial(_moba_kernel, tq=tq, ck=ck, hs=hs),
        grid=(N_ATTN_HEADS // hs, s // tq),
        in_specs=[tile, tile,
                  pl.BlockSpec((LANES, gw), lambda g, i: (0, g)),
                  pl.BlockSpec((s, LANES), lambda g, i: (0, 0), pipeline_mode=once),
                  pl.BlockSpec((s, gw), lambda g, i: (0, g), pipeline_mode=once),
                  pl.BlockSpec((s, gw), lambda g, i: (0, g), pipeline_mode=once)],
        out_specs=tile,
        out_shape=jax.ShapeDtypeStruct((s, ATTN_WIDTH), BF16),
        scratch_shapes=[pltpu.VMEM((hs, tq, 2 * HEAD_DIM), BF16)] +
                       [pltpu.VMEM((hs, tq, ck), F32)] * 2 +
                       [pltpu.VMEM((hs, tq, LANES), F32)] * 3,
        compiler_params=_params(2), name="moba",
    )(qs, qf, km_pad, onehot, kb, vb)


def _memkv_kernel(mem_ref, w_ref, kv_ref):
    kv_ref[...] = jnp.dot(mem_ref[...].astype(BF16), w_ref[...],
                          preferred_element_type=F32).astype(BF16)


def _postmix_kernel(attn_ref, sgu_ref, x_ref, wout_ref, g1_ref, b1_ref, wmq_ref, kv_ref,
                    wmo_ref, g2_ref, b2_ref, o_ref):
    mixin = jnp.concatenate([attn_ref[...], sgu_ref[...]], axis=1)
    mix = jnp.dot(mixin, wout_ref[...], preferred_element_type=F32)
    y1 = _layer_norm(DEEPNORM_ALPHA * x_ref[...] + mix, g1_ref[...], b1_ref[...])

    q = jnp.dot(y1.astype(BF16), wmq_ref[...], preferred_element_type=F32)
    q = (q * (MEM_HEAD_DIM ** -0.5)).astype(BF16)
    width = MEM_HEADS * MEM_HEAD_DIM
    heads = []
    for h in range(MEM_HEADS):
        sl = slice(h * MEM_HEAD_DIM, (h + 1) * MEM_HEAD_DIM)
        sc = lax.dot_general(q[:, sl], kv_ref[:, sl], _NT, preferred_element_type=F32)
        p = jnp.exp(sc - jnp.max(sc, axis=-1, keepdims=True))
        denom = jnp.sum(p, axis=-1, keepdims=True)
        vsl = slice(width + h * MEM_HEAD_DIM, width + (h + 1) * MEM_HEAD_DIM)
        heads.append(jnp.dot(p.astype(BF16), kv_ref[:, vsl], preferred_element_type=F32) / denom)
    o = jnp.concatenate(heads, axis=1).astype(BF16)
    xat = jnp.dot(o, wmo_ref[...], preferred_element_type=F32)
    o_ref[...] = _layer_norm(DEEPNORM_ALPHA * y1 + xat, g2_ref[...], b2_ref[...])


def _postmix(attn, sgu, x2d, mem2d, w_out, ln1_g, ln1_b, w_mq, w_mkv, w_mo, ln2_g, ln2_b, tm):
    s, d = x2d.shape
    m = mem2d.shape[0]
    width = MEM_HEADS * MEM_HEAD_DIM
    kv = pl.pallas_call(
        _memkv_kernel,
        out_shape=jax.ShapeDtypeStruct((m, 2 * width), BF16),
        compiler_params=_params(0), name="mem_kv",
    )(mem2d, w_mkv.astype(BF16))

    def full(shape):
        return pl.BlockSpec(shape, lambda i: (0,) * len(shape))

    def rows(cols):
        return pl.BlockSpec((tm, cols), lambda i: (i, 0))

    return pl.pallas_call(
        _postmix_kernel, grid=(s // tm,),
        in_specs=[rows(ATTN_WIDTH), rows(SGU_WIDTH), rows(d), full((ATTN_WIDTH + SGU_WIDTH, d)),
                  full((1, d)), full((1, d)), full((d, width)), full((m, 2 * width)),
                  full((width, d)), full((1, d)), full((1, d))],
        out_specs=rows(d),
        out_shape=jax.ShapeDtypeStruct((s, d), F32),
        compiler_params=_params(1), name="postmix",
    )(attn, sgu, x2d, w_out.astype(BF16), ln1_g.reshape(1, d), ln1_b.reshape(1, d),
      w_mq.astype(BF16), kv, w_mo.astype(BF16), ln2_g.reshape(1, d), ln2_b.reshape(1, d))


def _router_kernel(x_ref, wh_ref, wl_ref, bias_ref, eidx_ref, gw_ref, cnt_ref):
    tm = x_ref.shape[0]
    x = x_ref[...]
    xh = x.astype(BF16)
    xl = (x - xh.astype(F32)).astype(BF16)
    wh = wh_ref[...]
    logits = (lax.dot_general(wh, xh, _NT, preferred_element_type=F32)
              + lax.dot_general(wh, xl, _NT, preferred_element_type=F32)
              + lax.dot_general(wl_ref[...], xh, _NT, preferred_element_type=F32))
    scores = jax.nn.sigmoid(logits)
    biased = scores + bias_ref[...]

    g3 = biased.reshape(N_EXPERT_GROUPS, GROUP_SIZE, tm)
    in_grp = lax.broadcasted_iota(jnp.int32, g3.shape, 1).astype(F32)
    top1 = jnp.max(g3, axis=1, keepdims=True)
    first = jnp.min(jnp.where(g3 == top1, in_grp, float(GROUP_SIZE)), axis=1, keepdims=True)
    top2 = jnp.max(jnp.where(in_grp == first, -jnp.inf, g3), axis=1, keepdims=True)
    grp_score = (top1 + top2).reshape(N_EXPERT_GROUPS, tm)

    gid = lax.broadcasted_iota(jnp.int32, grp_score.shape, 0).astype(F32)
    grp_keep = jnp.zeros(grp_score.shape, F32)
    for _ in range(TOPK_GROUPS):
        best = jnp.max(grp_score, axis=0, keepdims=True)
        first = jnp.min(jnp.where(grp_score == best, gid, float(N_EXPERT_GROUPS)),
                        axis=0, keepdims=True)
        pick = gid == first
        grp_keep = jnp.where(pick, 1.0, grp_keep)
        grp_score = jnp.where(pick, -jnp.inf, grp_score)
    keep = jnp.broadcast_to(grp_keep.reshape(N_EXPERT_GROUPS, 1, tm), g3.shape) > 0.5
    masked = jnp.where(keep, g3, -jnp.inf).reshape(N_EXPERTS, tm)

    eid = lax.broadcasted_iota(jnp.int32, masked.shape, 0).astype(F32)
    idx_rows, w_rows = [], []
    chosen = jnp.zeros(masked.shape, F32)
    for _ in range(TOP_K):
        best = jnp.max(masked, axis=0, keepdims=True)
        first = jnp.min(jnp.where(masked == best, eid, float(N_EXPERTS)), axis=0, keepdims=True)
        pick = eid == first
        idx_rows.append(first)
        w_rows.append(jnp.sum(jnp.where(pick, scores, 0.0), axis=0, keepdims=True))
        masked = jnp.where(pick, -jnp.inf, masked)
        chosen = jnp.where(pick, 1.0, chosen)
    gw = jnp.concatenate(w_rows, axis=0)
    gw = gw / jnp.sum(gw, axis=0, keepdims=True) * ROUTED_SCALE
    eidx_ref[...] = jnp.concatenate(idx_rows, axis=0).astype(jnp.int32)
    gw_ref[...] = gw

    @pl.when(pl.program_id(0) == 0)
    def _():
        cnt_ref[...] = jnp.zeros(cnt_ref.shape, F32)

    cnt_ref[...] += jnp.sum(chosen, axis=1, keepdims=True)


def _router(x2, w_router, router_bias, tm):
    t, d = x2.shape
    wr_t = w_router.T.astype(F32)
    wr_hi = wr_t.astype(BF16)
    wr_lo = (wr_t - wr_hi.astype(F32)).astype(BF16)
    return pl.pallas_call(
        _router_kernel, grid=(t // tm,),
        in_specs=[pl.BlockSpec((tm, d), lambda i: (i, 0)),
                  pl.BlockSpec((N_EXPERTS, d), lambda i: (0, 0)),
                  pl.BlockSpec((N_EXPERTS, d), lambda i: (0, 0)),
                  pl.BlockSpec((N_EXPERTS, 1), lambda i: (0, 0))],
        out_specs=[pl.BlockSpec((TOP_K, tm), lambda i: (0, i)),
                   pl.BlockSpec((TOP_K, tm), lambda i: (0, i)),
                   pl.BlockSpec((N_EXPERTS, 1), lambda i: (0, 0))],
        out_shape=[jax.ShapeDtypeStruct((TOP_K, t), jnp.int32),
                   jax.ShapeDtypeStruct((TOP_K, t), F32),
                   jax.ShapeDtypeStruct((N_EXPERTS, 1), F32)],
        compiler_params=_params(1), name="router",
    )(x2, wr_hi, wr_lo, router_bias.reshape(N_EXPERTS, 1).astype(F32))


IDS_WINDOW = 2048
IDS_ALIGN = 1024
X_SLOTS, Y_SLOTS, ID_SLOTS = 3, 2, 4


def _expert_kernel(blk_e_ref, jwin_ref, joff_ref, nvalid_ref, nused_ref,
                   ids_hbm, x_hbm, w1_ref, w3_ref, w2_ref, y_hbm,
                   ids_smem, xbuf, ybuf, isem, gsem, ssem, *, n_sorted):
    b = pl.program_id(0)
    nused = nused_ref[0]
    rows = MOE_BLOCK
    spare0 = y_hbm.shape[0] - Y_SLOTS * rows

    def ids_copies(blk):
        slot = blk % ID_SLOTS
        src0 = pl.multiple_of(jwin_ref[blk] * IDS_ALIGN, IDS_ALIGN)
        dst0 = pl.multiple_of(slot * 2 * IDS_WINDOW, IDS_WINDOW)
        return [pltpu.make_async_copy(ids_hbm.at[pl.ds(src0 + part * n_sorted, IDS_WINDOW)],
                                      ids_smem.at[pl.ds(dst0 + part * IDS_WINDOW, IDS_WINDOW)],
                                      isem.at[slot]) for part in range(2)]

    def start_ids(blk):
        for c in ids_copies(blk):
            c.start()

    def wait_ids(blk):
        for c in ids_copies(blk):
            c.wait()

    def start_gather(blk):
        slot = blk % X_SLOTS
        base = (blk % ID_SLOTS) * 2 * IDS_WINDOW + joff_ref[blk]
        for r in range(rows):
            tok = ids_smem[base + r]
            pltpu.make_async_copy(x_hbm.at[pl.ds(tok, 1)], xbuf.at[slot, pl.ds(r, 1)],
                                  gsem.at[slot]).start()

    def wait_gather(blk):
        slot = blk % X_SLOTS
        pltpu.make_async_copy(x_hbm.at[pl.ds(0, rows)], xbuf.at[slot], gsem.at[slot]).wait()

    def start_scatter(blk):
        slot = blk % Y_SLOTS
        base = (blk % ID_SLOTS) * 2 * IDS_WINDOW + IDS_WINDOW + joff_ref[blk]
        nvalid = nvalid_ref[blk]
        spare = spare0 + slot * rows
        for r in range(rows):
            dst = jnp.where(r < nvalid, ids_smem[base + r], spare + r)
            pltpu.make_async_copy(ybuf.at[slot, pl.ds(r, 1)], y_hbm.at[pl.ds(dst, 1)],
                                  ssem.at[slot]).start()

    def wait_scatter(blk):
        slot = blk % Y_SLOTS
        pltpu.make_async_copy(ybuf.at[slot], y_hbm.at[pl.ds(0, rows)], ssem.at[slot]).wait()

    @pl.when(b == 0)
    def _():
        for blk in range(3):
            start_ids(blk)
        ybuf[0] = jnp.zeros(ybuf.shape[1:], F32)
        for half in range(Y_SLOTS):
            spare = pltpu.make_async_copy(ybuf.at[0], y_hbm.at[pl.ds(spare0 + half * rows, rows)],
                                          ssem.at[half])
            spare.start()
            spare.wait()
        for blk in range(2):
            wait_ids(blk)
            start_gather(blk)

    @pl.when(b < nused)
    def _():
        wait_ids(b + 2)
        wait_gather(b)

        @pl.when(b >= Y_SLOTS)
        def _():
            wait_scatter(b)

        xb = xbuf[b % X_SLOTS]
        h = _silu(jnp.dot(xb, w1_ref[...], preferred_element_type=F32)) * jnp.dot(
            xb, w3_ref[...], preferred_element_type=F32)
        ybuf[b % Y_SLOTS] = jnp.dot(h, w2_ref[...], preferred_element_type=F32)
        start_scatter(b)
        start_gather(b + 2)
        start_ids(b + 3)

        @pl.when(b == nused - 1)
        def _():
            wait_scatter(b)

            @pl.when(b >= 1)
            def _():
                wait_scatter(b - 1)

            wait_gather(b + 1)
            wait_gather(b + 2)
            wait_ids(b + 3)


def _routed_experts(x2, eidx, counts, w1, w3, w2):
    t, d = x2.shape
    e, _, f = w1.shape
    m = t * TOP_K
    flat_e = eidx.reshape(m)
    _, order = lax.sort_key_val(flat_e, jnp.arange(m, dtype=jnp.int32))
    tok_sorted = order // TOP_K
    dst_sorted = (order % TOP_K) * t + tok_sorted
    n_sorted = m + IDS_WINDOW
    pad = jnp.zeros((IDS_WINDOW,), jnp.int32)
    ids = jnp.concatenate([tok_sorted, pad, dst_sorted, pad])

    counts = counts.reshape(e).astype(jnp.int32)
    ends = jnp.cumsum(counts)
    starts = ends - counts
    pcounts = (counts + MOE_BLOCK - 1) // MOE_BLOCK * MOE_BLOCK
    pends = jnp.cumsum(pcounts)
    pstarts = pends - pcounts
    n_blk = -(-m // MOE_BLOCK) + e
    nused = (pends[-1] // MOE_BLOCK).astype(jnp.int32)
    n_meta = n_blk + ID_SLOTS
    row0 = jnp.arange(n_meta, dtype=jnp.int32) * MOE_BLOCK
    blk_e = jnp.minimum(jnp.sum(row0[:, None] >= pends[None, :], axis=1), e - 1).astype(jnp.int32)
    live = jnp.arange(n_meta) < nused
    onehot = blk_e[:, None] == jnp.arange(e)[None, :]
    pick = lambda v: jnp.sum(jnp.where(onehot, v[None, :], 0), axis=1)
    in_seg = row0 - pick(pstarts)
    jstart = jnp.where(live, pick(starts) + in_seg, 0)
    nvalid = jnp.where(live, jnp.clip(pick(counts) - in_seg, 0, MOE_BLOCK), 0).astype(jnp.int32)
    last_e = jnp.sum(jnp.where(jnp.arange(n_meta) == nused - 1, blk_e, 0))
    blk_e = jnp.where(live, blk_e, last_e).astype(jnp.int32)
    jwin = (jstart // IDS_ALIGN).astype(jnp.int32)
    joff = (jstart % IDS_ALIGN).astype(jnp.int32)

    grid_spec = pltpu.PrefetchScalarGridSpec(
        num_scalar_prefetch=5, grid=(n_blk,),
        in_specs=[pl.BlockSpec(memory_space=pl.ANY),
                  pl.BlockSpec(memory_space=pl.ANY),
                  pl.BlockSpec((None, d, f), lambda b, be, *_: (be[b], 0, 0)),
                  pl.BlockSpec((None, d, f), lambda b, be, *_: (be[b], 0, 0)),
                  pl.BlockSpec((None, f, d), lambda b, be, *_: (be[b], 0, 0))],
        out_specs=pl.BlockSpec(memory_space=pl.ANY),
        scratch_shapes=[pltpu.SMEM((ID_SLOTS * 2 * IDS_WINDOW,), jnp.int32),
                        pltpu.VMEM((X_SLOTS, MOE_BLOCK, d), F32),
                        pltpu.VMEM((Y_SLOTS, MOE_BLOCK, d), F32),
                        pltpu.SemaphoreType.DMA((ID_SLOTS,)),
                        pltpu.SemaphoreType.DMA((X_SLOTS,)),
                        pltpu.SemaphoreType.DMA((Y_SLOTS,))])
    return pl.pallas_call(
        functools.partial(_expert_kernel, n_sorted=n_sorted), grid_spec=grid_spec,
        out_shape=jax.ShapeDtypeStruct((m + Y_SLOTS * MOE_BLOCK, d), F32),
        compiler_params=_params(1), name="experts",
    )(blk_e, jwin, joff, nvalid, nused.reshape(1), ids, x2, w1, w3, w2)


def _final_kernel(x_ref, gw_ref, *refs):
    y_refs = refs[:TOP_K]
    ws1_ref, ws3_ref, ws2_ref, g_ref, b_ref, o_ref = refs[TOP_K:]
    x = x_ref[...]
    xb = x.astype(BF16)
    hid = _silu(jnp.dot(xb, ws1_ref[...], preferred_element_type=F32)) * jnp.dot(
        xb, ws3_ref[...], preferred_element_type=F32)
    shared = jnp.dot(hid.astype(BF16), ws2_ref[...], preferred_element_type=F32)
    gw = gw_ref[...]
    routed = gw[:, 0:1] * y_refs[0][...]
    for k in range(1, TOP_K):
        routed = routed + gw[:, k:k + 1] * y_refs[k][...]
    o_ref[...] = _layer_norm(DEEPNORM_ALPHA * x + (routed + shared), g_ref[...], b_ref[...])


def _final(x2, gw, y, ws1, ws3, ws2, ln3_g, ln3_b, tm):
    t, d = x2.shape
    f = ws1.shape[1]
    nrow = t // tm

    def full(shape):
        return pl.BlockSpec(shape, lambda i: (0,) * len(shape))

    y_specs = [pl.BlockSpec((tm, d), functools.partial(lambda i, k: (k * nrow + i, 0), k=k))
               for k in range(TOP_K)]
    return pl.pallas_call(
        _final_kernel, grid=(nrow,),
        in_specs=[pl.BlockSpec((tm, d), lambda i: (i, 0)),
                  pl.BlockSpec((tm, TOP_K), lambda i: (i, 0))] + y_specs +
                 [full((d, f)), full((d, f)), full((f, d)), full((1, d)), full((1, d))],
        out_specs=pl.BlockSpec((tm, d), lambda i: (i, 0)),
        out_shape=jax.ShapeDtypeStruct((t, d), F32),
        compiler_params=_params(1), name="final",
    )(x2, gw, *([y] * TOP_K), ws1.astype(BF16), ws3.astype(BF16), ws2.astype(BF16),
      ln3_g.reshape(1, d), ln3_b.reshape(1, d))


def kernel(x, mem, w_in, sgu_ln_g, sgu_ln_b, w_s, b_s, w_out, ln1_g, ln1_b, w_mq, w_mkv, w_mo,
           ln2_g, ln2_b, w_router, router_bias, w1, w3, w2, ws1, ws3, ws2, ln3_g, ln3_b):
    b, s, d = x.shape
    assert b == 1
    h = x.reshape(s, d)
    mem2d = mem.reshape(mem.shape[1], d)
    for l in range(DEPTH):
        qs, qf, kb, kmean, vb, sgu = _projections(
            h, w_in[l].astype(BF16), sgu_ln_g[l], sgu_ln_b[l], w_s[l], b_s[l], tm=min(512, s))
        attn = _moba(qs, qf, kb, kmean, vb, ck=2 * MOBA_BLOCK, hs=2)
        x2 = _postmix(attn, sgu, h, mem2d, w_out[l], ln1_g[l], ln1_b[l], w_mq[l], w_mkv[l],
                      w_mo[l], ln2_g[l], ln2_b[l], tm=256)
        eidx_t, gw_t, counts = _router(x2, w_router[l], router_bias[l], tm=min(512, s))
        y = _routed_experts(x2, eidx_t.T, counts, w1[l], w3[l], w2[l])
        h = _final(x2, gw_t.T, y, ws1[l], ws3[l], ws2[l], ln3_g[l], ln3_b[l], tm=128)
    return h.reshape(b, s, d)
```

```python
import functools

import jax
import jax.numpy as jnp
from jax import lax
from jax.experimental import pallas as pl
from jax.experimental.pallas import tpu as pltpu

F32 = jnp.float32
BF16 = jnp.bfloat16

N_ATTN_HEADS = 8
HEAD_DIM = 128
ATTN_WIDTH = N_ATTN_HEADS * HEAD_DIM
N_SGU_GROUPS = 8
SGU_GROUP_DIM = 128
SGU_WIDTH = N_SGU_GROUPS * SGU_GROUP_DIM
SGU_CHUNK = 128
MOBA_BLOCK = 256
MOBA_TOPK = 3
ROPE_THETA = 500000.0
ROPE_DIM = HEAD_DIM // 4
MEM_HEADS = 4
MEM_HEAD_DIM = 128
N_EXPERTS = 256
TOP_K = 8
N_EXPERT_GROUPS = 8
GROUP_SIZE = N_EXPERTS // N_EXPERT_GROUPS
TOPK_GROUPS = 4
ROUTED_SCALE = 2.5
MOE_BLOCK = 128
LN_EPS = 1e-5
DEPTH = 1
DEEPNORM_ALPHA = (2 * DEPTH) ** 0.25

LANES = 128
MASK_NEG = -1e30
LOG2_E = 1.4426950408889634
VMEM_LIMIT = 56 * 1024 * 1024

_NT = (((1,), (1,)), ((), ()))


def _params(n_axes, vmem=VMEM_LIMIT):
    return pltpu.CompilerParams(dimension_semantics=("arbitrary",) * n_axes,
                                vmem_limit_bytes=vmem)


def _layer_norm(x, g, b):
    mu = jnp.mean(x, axis=-1, keepdims=True)
    xc = x - mu
    var = jnp.mean(xc * xc, axis=-1, keepdims=True)
    return xc * lax.rsqrt(var + LN_EPS) * g + b


def _silu(x):
    return x * jax.nn.sigmoid(x)


def _store_token_tiles(ref, x):
    rows, n = x.shape[0], x.shape[1] // LANES
    for c in range(n):
        ref[pl.ds(c, rows, stride=n), :] = x[:, c * LANES:(c + 1) * LANES]


def _load_token_tiles(ref, rows, n):
    return jnp.concatenate([ref[pl.ds(c, rows, stride=n), :] for c in range(n)], axis=1)


def _rope(acc, cosf, sinf):
    half = ROPE_DIM // 2
    rows = acc.shape[0]
    lane = lax.broadcasted_iota(jnp.int32, (rows, LANES), 1)
    outs = []
    for c in range(acc.shape[1] // LANES):
        t = acc[:, c * LANES:(c + 1) * LANES]
        partner = jnp.where(lane < half, pltpu.roll(t, LANES - half, 1), pltpu.roll(t, half, 1))
        outs.append(t * cosf + partner * sinf)
    return jnp.concatenate(outs, axis=1)


def _q_kernel(x_ref, w_ref, cos_ref, sin_ref, qs_ref, qf_ref):
    acc = jnp.dot(x_ref[...].astype(BF16), w_ref[...], preferred_element_type=F32)
    r = _rope(acc, cos_ref[...], sin_ref[...])
    qf_ref[...] = r
    qs_ref[...] = (r * (HEAD_DIM ** -0.5 * LOG2_E)).astype(BF16)


def _k_kernel(x_ref, w_ref, cos_ref, sin_ref, k_ref, km_ref):
    acc = jnp.dot(x_ref[...].astype(BF16), w_ref[...], preferred_element_type=F32)
    r = _rope(acc, cos_ref[...], sin_ref[...])
    k_ref[...] = r.astype(BF16)
    rows, cols = r.shape
    km_ref[0] = jnp.mean(r.reshape(rows // MOBA_BLOCK, MOBA_BLOCK, cols), axis=1)


def _v_kernel(x_ref, w_ref, v_ref):
    v_ref[...] = jnp.dot(x_ref[...].astype(BF16), w_ref[...],
                         preferred_element_type=F32).astype(BF16)


def _sgu_kernel(x_ref, wu_ref, wz_ref, g_ref, b_ref, ws_ref, bs_ref, o_ref):
    xb = x_ref[...].astype(BF16)
    u = jax.nn.gelu(jnp.dot(xb, wu_ref[...], preferred_element_type=F32))
    z = jax.nn.gelu(jnp.dot(xb, wz_ref[...], preferred_element_type=F32))
    rows, cols = u.shape
    t_row = lax.broadcasted_iota(jnp.int32, (SGU_CHUNK, SGU_CHUNK), 0)
    s_col = lax.broadcasted_iota(jnp.int32, (SGU_CHUNK, SGU_CHUNK), 1)
    causal = s_col <= t_row
    for gl in range(cols // SGU_GROUP_DIM):
        sl = slice(gl * SGU_GROUP_DIM, (gl + 1) * SGU_GROUP_DIM)
        zn = _layer_norm(z[:, sl], g_ref[:, sl], b_ref[:, sl]).astype(BF16)
        w = jnp.where(causal, ws_ref[gl], 0.0).astype(BF16)
        bias = bs_ref[:, gl:gl + 1]
        for c in range(rows // SGU_CHUNK):
            rs = slice(c * SGU_CHUNK, (c + 1) * SGU_CHUNK)
            mixed = jnp.dot(w, zn[rs], preferred_element_type=F32) + bias
            o_ref[rs, sl] = (u[rs, sl] * mixed).astype(BF16)


def _projections(x2d, w_in_bf, sgu_ln_g, sgu_ln_b, w_s, b_s, tm):
    s, d = x2d.shape
    tn = 512
    nrow = s // tm
    half = ROPE_DIM // 2

    pos = jnp.arange(s)
    inv_freq = ROPE_THETA ** (-jnp.arange(half, dtype=F32) / half)
    ang = pos.astype(F32)[:, None] * inv_freq[None, :]
    cos, sin = jnp.cos(ang), jnp.sin(ang)
    rest = LANES - ROPE_DIM
    cosf = jnp.concatenate([cos, cos, jnp.ones((s, rest), F32)], axis=1)
    sinf = jnp.concatenate([-sin, sin, jnp.zeros((s, rest), F32)], axis=1)

    x_spec = pl.BlockSpec((tm, d), lambda i, j: (i, 0))
    tab_spec = pl.BlockSpec((tm, LANES), lambda i, j: (i, 0))
    out_spec = pl.BlockSpec((tm, tn), lambda i, j: (i, j))
    nq = ATTN_WIDTH // tn

    def w_spec(col0):
        return pl.BlockSpec((d, tn), lambda i, j: (0, col0 // tn + j))

    qs, qf = pl.pallas_call(
        _q_kernel, grid=(nrow, nq),
        in_specs=[x_spec, w_spec(0), tab_spec, tab_spec],
        out_specs=[out_spec, out_spec],
        out_shape=[jax.ShapeDtypeStruct((s, ATTN_WIDTH), BF16),
                   jax.ShapeDtypeStruct((s, ATTN_WIDTH), F32)],
        compiler_params=_params(2), name="proj_q",
    )(x2d, w_in_bf, cosf, sinf)

    kpb = tm // MOBA_BLOCK
    kb, kmean = pl.pallas_call(
        _k_kernel, grid=(nrow, nq),
        in_specs=[x_spec, w_spec(ATTN_WIDTH), tab_spec, tab_spec],
        out_specs=[out_spec, pl.BlockSpec((1, kpb, tn), lambda i, j: (i, 0, j))],
        out_shape=[jax.ShapeDtypeStruct((s, ATTN_WIDTH), BF16),
                   jax.ShapeDtypeStruct((nrow, kpb, ATTN_WIDTH), F32)],
        compiler_params=_params(2), name="proj_k",
    )(x2d, w_in_bf, cosf, sinf)
    kmean = kmean.reshape(s // MOBA_BLOCK, ATTN_WIDTH)

    vb = pl.pallas_call(
        _v_kernel, grid=(nrow, nq),
        in_specs=[x_spec, w_spec(2 * ATTN_WIDTH)],
        out_specs=out_spec,
        out_shape=jax.ShapeDtypeStruct((s, ATTN_WIDTH), BF16),
        compiler_params=_params(2), name="proj_v",
    )(x2d, w_in_bf)

    gpb = tn // SGU_GROUP_DIM
    sgu = pl.pallas_call(
        _sgu_kernel, grid=(nrow, SGU_WIDTH // tn),
        in_specs=[x_spec, w_spec(3 * ATTN_WIDTH), w_spec(3 * ATTN_WIDTH + SGU_WIDTH),
                  pl.BlockSpec((1, tn), lambda i, j: (0, j)),
                  pl.BlockSpec((1, tn), lambda i, j: (0, j)),
                  pl.BlockSpec((gpb, SGU_CHUNK, SGU_CHUNK), lambda i, j: (j, 0, 0)),
                  pl.BlockSpec((None, SGU_CHUNK, gpb), lambda i, j: (j, 0, 0))],
        out_specs=out_spec,
        out_shape=jax.ShapeDtypeStruct((s, SGU_WIDTH), BF16),
        compiler_params=_params(2), name="proj_sgu",
    )(x2d, w_in_bf, w_in_bf, sgu_ln_g.reshape(1, SGU_WIDTH), sgu_ln_b.reshape(1, SGU_WIDTH),
      w_s, b_s.reshape(N_SGU_GROUPS // gpb, gpb, SGU_CHUNK).transpose(0, 2, 1))
    return qs, qf, kb, kmean, vb, sgu


def _moba_kernel(qs_ref, qf_ref, km_ref, oh_ref, k_ref, v_ref, o_ref,
                 qaug_scr, sa_scr, sb_scr, m_scr, l_scr, acc_scr, *, tq, ck, hs):
    i = pl.program_id(1)
    own = i
    n_full = own // (ck // MOBA_BLOCK)
    col = lax.broadcasted_iota(jnp.int32, (tq, LANES), 1)
    colf = col.astype(F32)

    for hh in range(hs):
        hsl = slice(hh * HEAD_DIM, (hh + 1) * HEAD_DIM)
        gate = lax.dot_general(qf_ref[:, hsl], km_ref[:, hsl], _NT,
                               precision=lax.Precision.HIGHEST,
                               preferred_element_type=F32)
        gate = jnp.where(col < own, gate, -jnp.inf)
        selbias = jnp.where(col == own, 0.0, MASK_NEG)
        for _ in range(MOBA_TOPK):
            gmax = jnp.max(gate, axis=-1, keepdims=True)
            cand = jnp.where(gate == gmax, colf, float(LANES))
            cand = jnp.where(gmax > -jnp.inf, cand, float(LANES))
            first = jnp.min(cand, axis=-1, keepdims=True)
            pick = colf == first
            selbias = jnp.where(pick, 0.0, selbias)
            gate = jnp.where(pick, -jnp.inf, gate)
        qaug_scr[hh] = jnp.concatenate([qs_ref[:, hsl], selbias.astype(BF16)], axis=1)

    def scores(c0, s_ref, causal=None):
        c0 = pl.multiple_of(c0, ck)
        onehot = oh_ref[pl.ds(c0, ck), :]
        for hh in range(hs):
            hsl = slice(hh * HEAD_DIM, (hh + 1) * HEAD_DIM)
            kaug = jnp.concatenate([k_ref[pl.ds(c0, ck), hsl], onehot], axis=1)
            sc = lax.dot_general(qaug_scr[hh], kaug, _NT, preferred_element_type=F32)
            if causal is not None:
                sc = jnp.where(causal, sc, MASK_NEG)
            s_ref[hh] = sc

    def absorb(c0, s_ref):
        c0 = pl.multiple_of(c0, ck)
        for hh in range(hs):
            hsl = slice(hh * HEAD_DIM, (hh + 1) * HEAD_DIM)
            sc = s_ref[hh]
            m_old = m_scr[hh]
            m_new = jnp.maximum(m_old, jnp.max(sc, axis=-1, keepdims=True))
            alpha = jnp.exp2(m_old - m_new)
            pn = jnp.exp2(sc - m_new[:, :1])
            l_scr[hh] = alpha * l_scr[hh] + jnp.sum(pn, axis=-1, keepdims=True)
            acc_scr[hh] = alpha * acc_scr[hh] + jnp.dot(
                pn.astype(BF16), v_ref[pl.ds(c0, ck), hsl], preferred_element_type=F32)
            m_scr[hh] = m_new

    m_scr[...] = jnp.full(m_scr.shape, MASK_NEG, F32)
    l_scr[...] = jnp.zeros(l_scr.shape, F32)
    acc_scr[...] = jnp.zeros(acc_scr.shape, F32)
    d0 = n_full * ck
    q_pos = i * tq + lax.broadcasted_iota(jnp.int32, (tq, ck), 0)
    k_pos = d0 + lax.broadcasted_iota(jnp.int32, (tq, ck), 1)
    scores(d0, sa_scr, k_pos <= q_pos)

    def pair(u, carry):
        prev0 = jnp.where(u == 0, d0, (2 * u - 1) * ck)
        absorb(prev0, sa_scr)
        scores(2 * u * ck, sb_scr)
        absorb(2 * u * ck, sb_scr)
        scores((2 * u + 1) * ck, sa_scr)
        return carry

    n_pairs = n_full // 2
    lax.fori_loop(0, n_pairs, pair, 0)
    last0 = jnp.where(n_pairs == 0, d0, (2 * n_pairs - 1) * ck)

    @pl.when(n_full % 2 == 1)
    def _():
        absorb(last0, sa_scr)
        scores((n_full - 1) * ck, sb_scr)
        absorb((n_full - 1) * ck, sb_scr)

    @pl.when(n_full % 2 == 0)
    def _():
        absorb(last0, sa_scr)

    for hh in range(hs):
        o_ref[:, hh * HEAD_DIM:(hh + 1) * HEAD_DIM] = (acc_scr[hh] / l_scr[hh]).astype(BF16)


def _moba(qs, qf, kb, kmean, vb, ck, hs):
    s = qs.shape[0]
    tq = MOBA_BLOCK
    nb = s // MOBA_BLOCK
    assert nb <= LANES and s % ck == 0 and ck % MOBA_BLOCK == 0 and N_ATTN_HEADS % hs == 0
    km_pad = jnp.zeros((LANES, ATTN_WIDTH), F32).at[:nb].set(kmean)
    onehot = (jnp.arange(s)[:, None] // MOBA_BLOCK == jnp.arange(LANES)[None, :]).astype(BF16)
    gw = hs * HEAD_DIM
    tile = pl.BlockSpec((tq, gw), lambda g, i: (i, g))
    once = pl.Buffered(1)
    return pl.pallas_call(
        functools.partial(_moba_kernel, tq=tq, ck=ck, hs=hs),
        grid=(N_ATTN_HEADS // hs, s // tq),
        in_specs=[tile, tile,
                  pl.BlockSpec((LANES, gw), lambda g, i: (0, g)),
                  pl.BlockSpec((s, LANES), lambda g, i: (0, 0), pipeline_mode=once),
                  pl.BlockSpec((s, gw), lambda g, i: (0, g), pipeline_mode=once),
                  pl.BlockSpec((s, gw), lambda g, i: (0, g), pipeline_mode=once)],
        out_specs=tile,
        out_shape=jax.ShapeDtypeStruct((s, ATTN_WIDTH), BF16),
        scratch_shapes=[pltpu.VMEM((hs, tq, 2 * HEAD_DIM), BF16)] +
                       [pltpu.VMEM((hs, tq, ck), F32)] * 2 +
                       [pltpu.VMEM((hs, tq, LANES), F32)] * 3,
        compiler_params=_params(2), name="moba",
    )(qs, qf, km_pad, onehot, kb, vb)


def _memkv_kernel(mem_ref, w_ref, kv_ref):
    kv_ref[...] = jnp.dot(mem_ref[...].astype(BF16), w_ref[...],
                          preferred_element_type=F32).astype(BF16)


def _postmix_kernel(attn_ref, sgu_ref, x_ref, wout_ref, g1_ref, b1_ref, wmq_ref, kv_ref,
                    wmo_ref, g2_ref, b2_ref, o_ref, op_ref):
    mixin = jnp.concatenate([attn_ref[...], sgu_ref[...]], axis=1)
    mix = jnp.dot(mixin, wout_ref[...], preferred_element_type=F32)
    y1 = _layer_norm(DEEPNORM_ALPHA * x_ref[...] + mix, g1_ref[...], b1_ref[...])

    q = jnp.dot(y1.astype(BF16), wmq_ref[...], preferred_element_type=F32)
    q = (q * (MEM_HEAD_DIM ** -0.5)).astype(BF16)
    width = MEM_HEADS * MEM_HEAD_DIM
    heads = []
    for h in range(MEM_HEADS):
        sl = slice(h * MEM_HEAD_DIM, (h + 1) * MEM_HEAD_DIM)
        sc = lax.dot_general(q[:, sl], kv_ref[:, sl], _NT, preferred_element_type=F32)
        p = jnp.exp(sc - jnp.max(sc, axis=-1, keepdims=True))
        denom = jnp.sum(p, axis=-1, keepdims=True)
        vsl = slice(width + h * MEM_HEAD_DIM, width + (h + 1) * MEM_HEAD_DIM)
        heads.append(jnp.dot(p.astype(BF16), kv_ref[:, vsl], preferred_element_type=F32) / denom)
    o = jnp.concatenate(heads, axis=1).astype(BF16)
    xat = jnp.dot(o, wmo_ref[...], preferred_element_type=F32)
    y2 = _layer_norm(DEEPNORM_ALPHA * y1 + xat, g2_ref[...], b2_ref[...])
    o_ref[...] = y2
    _store_token_tiles(op_ref, y2)


def _postmix(attn, sgu, x2d, mem2d, w_out, ln1_g, ln1_b, w_mq, w_mkv, w_mo, ln2_g, ln2_b, tm):
    s, d = x2d.shape
    m = mem2d.shape[0]
    width = MEM_HEADS * MEM_HEAD_DIM
    kv = pl.pallas_call(
        _memkv_kernel,
        out_shape=jax.ShapeDtypeStruct((m, 2 * width), BF16),
        compiler_params=_params(0), name="mem_kv",
    )(mem2d, w_mkv.astype(BF16))

    def full(shape):
        return pl.BlockSpec(shape, lambda i: (0,) * len(shape))

    def rows(cols):
        return pl.BlockSpec((tm, cols), lambda i: (i, 0))

    return pl.pallas_call(
        _postmix_kernel, grid=(s // tm,),
        in_specs=[rows(ATTN_WIDTH), rows(SGU_WIDTH), rows(d), full((ATTN_WIDTH + SGU_WIDTH, d)),
                  full((1, d)), full((1, d)), full((d, width)), full((m, 2 * width)),
                  full((width, d)), full((1, d)), full((1, d))],
        out_specs=[rows(d), pl.BlockSpec((tm * (d // LANES), LANES), lambda i: (i, 0))],
        out_shape=[jax.ShapeDtypeStruct((s, d), F32),
                   jax.ShapeDtypeStruct((s * (d // LANES), LANES), F32)],
        compiler_params=_params(1), name="postmix",
    )(attn, sgu, x2d, w_out.astype(BF16), ln1_g.reshape(1, d), ln1_b.reshape(1, d),
      w_mq.astype(BF16), kv, w_mo.astype(BF16), ln2_g.reshape(1, d), ln2_b.reshape(1, d))


def _router_kernel(x_ref, wh_ref, wl_ref, bias_ref, eidx_ref, gw_ref, cnt_ref):
    tm = x_ref.shape[0]
    x = x_ref[...]
    xh = x.astype(BF16)
    xl = (x - xh.astype(F32)).astype(BF16)
    wh = wh_ref[...]
    logits = (lax.dot_general(wh, xh, _NT, preferred_element_type=F32)
              + lax.dot_general(wh, xl, _NT, preferred_element_type=F32)
              + lax.dot_general(wl_ref[...], xh, _NT, preferred_element_type=F32))
    scores = jax.nn.sigmoid(logits)
    biased = scores + bias_ref[...]

    g3 = biased.reshape(N_EXPERT_GROUPS, GROUP_SIZE, tm)
    in_grp = lax.broadcasted_iota(jnp.int32, g3.shape, 1).astype(F32)
    top1 = jnp.max(g3, axis=1, keepdims=True)
    first = jnp.min(jnp.where(g3 == top1, in_grp, float(GROUP_SIZE)), axis=1, keepdims=True)
    top2 = jnp.max(jnp.where(in_grp == first, -jnp.inf, g3), axis=1, keepdims=True)
    grp_score = (top1 + top2).reshape(N_EXPERT_GROUPS, tm)

    gid = lax.broadcasted_iota(jnp.int32, grp_score.shape, 0).astype(F32)
    grp_keep = jnp.zeros(grp_score.shape, F32)
    for _ in range(TOPK_GROUPS):
        best = jnp.max(grp_score, axis=0, keepdims=True)
        first = jnp.min(jnp.where(grp_score == best, gid, float(N_EXPERT_GROUPS)),
                        axis=0, keepdims=True)
        pick = gid == first
        grp_keep = jnp.where(pick, 1.0, grp_keep)
        grp_score = jnp.where(pick, -jnp.inf, grp_score)
    keep = jnp.broadcast_to(grp_keep.reshape(N_EXPERT_GROUPS, 1, tm), g3.shape) > 0.5
    masked = jnp.where(keep, g3, -jnp.inf).reshape(N_EXPERTS, tm)

    eid = lax.broadcasted_iota(jnp.int32, masked.shape, 0).astype(F32)
    idx_rows, w_rows = [], []
    chosen = jnp.zeros(masked.shape, F32)
    for _ in range(TOP_K):
        best = jnp.max(masked, axis=0, keepdims=True)
        first = jnp.min(jnp.where(masked == best, eid, float(N_EXPERTS)), axis=0, keepdims=True)
        pick = eid == first
        idx_rows.append(first)
        w_rows.append(jnp.sum(jnp.where(pick, scores, 0.0), axis=0, keepdims=True))
        masked = jnp.where(pick, -jnp.inf, masked)
        chosen = jnp.where(pick, 1.0, chosen)
    gw = jnp.concatenate(w_rows, axis=0)
    gw = gw / jnp.sum(gw, axis=0, keepdims=True) * ROUTED_SCALE
    eidx_ref[...] = jnp.concatenate(idx_rows, axis=0).astype(jnp.int32)
    gw_ref[...] = gw

    @pl.when(pl.program_id(0) == 0)
    def _():
        cnt_ref[...] = jnp.zeros(cnt_ref.shape, F32)

    cnt_ref[...] += jnp.sum(chosen, axis=1, keepdims=True)


def _router(x2, w_router, router_bias, tm):
    t, d = x2.shape
    wr_t = w_router.T.astype(F32)
    wr_hi = wr_t.astype(BF16)
    wr_lo = (wr_t - wr_hi.astype(F32)).astype(BF16)
    return pl.pallas_call(
        _router_kernel, grid=(t // tm,),
        in_specs=[pl.BlockSpec((tm, d), lambda i: (i, 0)),
                  pl.BlockSpec((N_EXPERTS, d), lambda i: (0, 0)),
                  pl.BlockSpec((N_EXPERTS, d), lambda i: (0, 0)),
                  pl.BlockSpec((N_EXPERTS, 1), lambda i: (0, 0))],
        out_specs=[pl.BlockSpec((TOP_K, tm), lambda i: (0, i)),
                   pl.BlockSpec((TOP_K, tm), lambda i: (0, i)),
                   pl.BlockSpec((N_EXPERTS, 1), lambda i: (0, 0))],
        out_shape=[jax.ShapeDtypeStruct((TOP_K, t), jnp.int32),
                   jax.ShapeDtypeStruct((TOP_K, t), F32),
                   jax.ShapeDtypeStruct((N_EXPERTS, 1), F32)],
        compiler_params=_params(1), name="router",
    )(x2, wr_hi, wr_lo, router_bias.reshape(N_EXPERTS, 1).astype(F32))


IDS_WINDOW = 2048
IDS_ALIGN = 1024
X_SLOTS, Y_SLOTS, ID_SLOTS = 4, 2, 8
DMA_GROUPS = 8


def _expert_kernel(bstart_ref, bcount_ref, jwin_ref, joff_ref, nvalid_ref, nused_ref,
                   ids_hbm, x_hbm, w1_ref, w3_ref, w2_ref, y_hbm,
                   ids_smem, xbuf, ybuf, isem, gsem, ssem, *, n_sorted):
    e = pl.program_id(0)
    nused = nused_ref[0]
    rows = MOE_BLOCK
    tpr = w1_ref.shape[0] // LANES
    spare0 = y_hbm.shape[0] - Y_SLOTS * rows

    def x_row(tok):
        return x_hbm.at[pl.ds(pl.multiple_of(tok * tpr, tpr), tpr)]

    def ids_copies(blk, slot):
        src0 = pl.multiple_of(jwin_ref[blk] * IDS_ALIGN, IDS_ALIGN)
        dst0 = pl.multiple_of(slot * 2 * IDS_WINDOW, IDS_WINDOW)
        return [pltpu.make_async_copy(ids_hbm.at[pl.ds(src0 + part * n_sorted, IDS_WINDOW)],
                                      ids_smem.at[pl.ds(dst0 + part * IDS_WINDOW, IDS_WINDOW)],
                                      isem.at[slot]) for part in range(2)]

    def start_ids(blk, slot):
        for c in ids_copies(blk, slot):
            c.start()

    def wait_ids(blk):
        for c in ids_copies(blk, blk % ID_SLOTS):
            c.wait()

    def gather_rows(blk, slot, r0, r1):
        base = (blk % ID_SLOTS) * 2 * IDS_WINDOW + joff_ref[blk]
        for r in range(r0, r1):
            tok = ids_smem[base + r]
            pltpu.make_async_copy(x_row(tok), xbuf.at[slot, pl.ds(r * tpr, tpr)],
                                  gsem.at[slot]).start(priority=r % 2)

    def wait_gather(blk):
        slot = blk % X_SLOTS
        pltpu.make_async_copy(x_hbm.at[pl.ds(0, rows * tpr)], xbuf.at[slot],
                              gsem.at[slot]).wait()

    def scatter_rows(blk, nvalid, slot, r0, r1):
        base = (blk % ID_SLOTS) * 2 * IDS_WINDOW + IDS_WINDOW + joff_ref[blk]
        spare = spare0 + slot * rows
        for r in range(r0, r1):
            dst = jnp.where(r < nvalid, ids_smem[base + r], spare + r)
            pltpu.make_async_copy(ybuf.at[slot, pl.ds(r, 1)], y_hbm.at[pl.ds(dst, 1)],
                                  ssem.at[slot]).start(priority=r % 2)

    def wait_scatter(slot):
        pltpu.make_async_copy(ybuf.at[slot], y_hbm.at[pl.ds(0, rows)], ssem.at[slot]).wait()

    def block_body(phase, g):
        ys, ps, ahead = phase % Y_SLOTS, (phase + 1) % Y_SLOTS, (phase + 2) % X_SLOTS
        prev = jnp.maximum(g - 1, 0)
        nvalid_prev = jnp.where(g > 0, nvalid_ref[prev], 0)
        per = rows // DMA_GROUPS
        groups = iter(range(DMA_GROUPS))

        def dma_group():
            k = next(groups)
            gather_rows(g + 2, ahead, k * per, (k + 1) * per)
            scatter_rows(prev, nvalid_prev, ps, k * per, (k + 1) * per)

        xb = _load_token_tiles(xbuf.at[phase], rows, tpr)
        nh = w1_ref.shape[1] // 2
        parts = []
        for w_ref in (w1_ref, w3_ref):
            for c in range(2):
                parts.append(jnp.dot(xb, w_ref[:, c * nh:(c + 1) * nh],
                                     preferred_element_type=F32))
                dma_group()
        h = _silu(jnp.concatenate(parts[:2], axis=1)) * jnp.concatenate(parts[2:], axis=1)
        ny = w2_ref.shape[1] // 4
        for c in range(4):
            ybuf[ys, :, c * ny:(c + 1) * ny] = jnp.dot(h, w2_ref[:, c * ny:(c + 1) * ny],
                                                       preferred_element_type=F32)
            dma_group()
        return 0

    @pl.when(e == 0)
    def _():
        for blk in range(4):
            start_ids(blk, blk)
        ybuf[...] = jnp.zeros(ybuf.shape, F32)
        for s in range(Y_SLOTS):
            spare = pltpu.make_async_copy(ybuf.at[0], y_hbm.at[pl.ds(spare0 + s * rows, rows)],
                                          ssem.at[s])
            spare.start()
            spare.wait()
        for blk in range(2):
            wait_ids(blk)
            gather_rows(blk, blk, 0, rows)

    def block(j, carry):
        g = bstart_ref[e] + j
        start_ids(g + 4, (g + 4) % ID_SLOTS)
        wait_ids(g + 2)
        wait_gather(g)

        @pl.when(g >= 1)
        def _():
            wait_scatter(g % Y_SLOTS)

        lax.switch(g % X_SLOTS, [functools.partial(block_body, ph) for ph in range(X_SLOTS)], g)
        return carry

    lax.fori_loop(0, bcount_ref[e], block, 0)

    @pl.when(e == pl.num_programs(0) - 1)
    def _():
        last = nused - 1

        def scatter_last(slot, _):
            scatter_rows(last, nvalid_ref[last], slot, 0, rows)
            return 0

        lax.switch(last % Y_SLOTS, [functools.partial(scatter_last, s) for s in range(Y_SLOTS)], 0)
        for s in range(Y_SLOTS):
            wait_scatter(s)
        for blk in range(1, 3):
            wait_gather(last + blk)
        for blk in range(3, 5):
            wait_ids(last + blk)


def _routed_experts(xt, eidx, counts, w1, w3, w2):
    e, d, f = w1.shape
    tpr = d // LANES
    t = xt.shape[0] // tpr
    m = t * TOP_K
    flat_e = eidx.reshape(m)
    _, order = lax.sort_key_val(flat_e, jnp.arange(m, dtype=jnp.int32))
    tok_sorted = order // TOP_K
    dst_sorted = (order % TOP_K) * t + tok_sorted
    n_sorted = m + IDS_WINDOW
    pad = jnp.zeros((IDS_WINDOW,), jnp.int32)
    ids = jnp.concatenate([tok_sorted, pad, dst_sorted, pad])

    counts = counts.reshape(e).astype(jnp.int32)
    ends = jnp.cumsum(counts)
    starts = ends - counts
    pcounts = (counts + MOE_BLOCK - 1) // MOE_BLOCK * MOE_BLOCK
    pends = jnp.cumsum(pcounts)
    pstarts = pends - pcounts
    n_blk = -(-m // MOE_BLOCK) + e
    nused = (pends[-1] // MOE_BLOCK).astype(jnp.int32)
    n_meta = n_blk + ID_SLOTS
    row0 = jnp.arange(n_meta, dtype=jnp.int32) * MOE_BLOCK
    blk_e = jnp.minimum(jnp.sum(row0[:, None] >= pends[None, :], axis=1), e - 1).astype(jnp.int32)
    live = jnp.arange(n_meta) < nused
    onehot = blk_e[:, None] == jnp.arange(e)[None, :]
    pick = lambda v: jnp.sum(jnp.where(onehot, v[None, :], 0), axis=1)
    in_seg = row0 - pick(pstarts)
    jstart = jnp.where(live, pick(starts) + in_seg, 0)
    nvalid = jnp.where(live, jnp.clip(pick(counts) - in_seg, 0, MOE_BLOCK), 0).astype(jnp.int32)
    jwin = (jstart // IDS_ALIGN).astype(jnp.int32)
    joff = (jstart % IDS_ALIGN).astype(jnp.int32)
    bstart = (pstarts // MOE_BLOCK).astype(jnp.int32)
    bcount = (pcounts // MOE_BLOCK).astype(jnp.int32)

    grid_spec = pltpu.PrefetchScalarGridSpec(
        num_scalar_prefetch=6, grid=(e,),
        in_specs=[pl.BlockSpec(memory_space=pl.ANY),
                  pl.BlockSpec(memory_space=pl.ANY),
                  pl.BlockSpec((None, d, f), lambda i, *_: (i, 0, 0)),
                  pl.BlockSpec((None, d, f), lambda i, *_: (i, 0, 0)),
                  pl.BlockSpec((None, f, d), lambda i, *_: (i, 0, 0))],
        out_specs=pl.BlockSpec(memory_space=pl.ANY),
        scratch_shapes=[pltpu.SMEM((ID_SLOTS * 2 * IDS_WINDOW,), jnp.int32),
                        pltpu.VMEM((X_SLOTS, MOE_BLOCK * tpr, LANES), F32),
                        pltpu.VMEM((Y_SLOTS, MOE_BLOCK, d), F32),
                        pltpu.SemaphoreType.DMA((ID_SLOTS,)),
                        pltpu.SemaphoreType.DMA((X_SLOTS,)),
                        pltpu.SemaphoreType.DMA((Y_SLOTS,))])
    return pl.pallas_call(
        functools.partial(_expert_kernel, n_sorted=n_sorted), grid_spec=grid_spec,
        out_shape=jax.ShapeDtypeStruct((m + Y_SLOTS * MOE_BLOCK, d), F32),
        compiler_params=_params(1), name="experts",
    )(bstart, bcount, jwin, joff, nvalid, nused.reshape(1), ids, xt, w1, w3, w2)


def _final_kernel(x_ref, gw_ref, *refs):
    y_refs = refs[:TOP_K]
    ws1_ref, ws3_ref, ws2_ref, g_ref, b_ref, o_ref = refs[TOP_K:]
    x = x_ref[...]
    xb = x.astype(BF16)
    hid = _silu(jnp.dot(xb, ws1_ref[...], preferred_element_type=F32)) * jnp.dot(
        xb, ws3_ref[...], preferred_element_type=F32)
    shared = jnp.dot(hid.astype(BF16), ws2_ref[...], preferred_element_type=F32)
    gw = gw_ref[...]
    routed = gw[:, 0:1] * y_refs[0][...]
    for k in range(1, TOP_K):
        routed = routed + gw[:, k:k + 1] * y_refs[k][...]
    o_ref[...] = _layer_norm(DEEPNORM_ALPHA * x + (routed + shared), g_ref[...], b_ref[...])


def _final(x2, gw, y, ws1, ws3, ws2, ln3_g, ln3_b, tm):
    t, d = x2.shape
    f = ws1.shape[1]
    nrow = t // tm

    def full(shape):
        return pl.BlockSpec(shape, lambda i: (0,) * len(shape))

    y_specs = [pl.BlockSpec((tm, d), functools.partial(lambda i, k: (k * nrow + i, 0), k=k))
               for k in range(TOP_K)]
    return pl.pallas_call(
        _final_kernel, grid=(nrow,),
        in_specs=[pl.BlockSpec((tm, d), lambda i: (i, 0)),
                  pl.BlockSpec((tm, TOP_K), lambda i: (i, 0))] + y_specs +
                 [full((d, f)), full((d, f)), full((f, d)), full((1, d)), full((1, d))],
        out_specs=pl.BlockSpec((tm, d), lambda i: (i, 0)),
        out_shape=jax.ShapeDtypeStruct((t, d), F32),
        compiler_params=_params(1), name="final",
    )(x2, gw, *([y] * TOP_K), ws1.astype(BF16), ws3.astype(BF16), ws2.astype(BF16),
      ln3_g.reshape(1, d), ln3_b.reshape(1, d))


def kernel(x, mem, w_in, sgu_ln_g, sgu_ln_b, w_s, b_s, w_out, ln1_g, ln1_b, w_mq, w_mkv, w_mo,
           ln2_g, ln2_b, w_router, router_bias, w1, w3, w2, ws1, ws3, ws2, ln3_g, ln3_b):
    b, s, d = x.shape
    assert b == 1
    h = x.reshape(s, d)
    mem2d = mem.reshape(mem.shape[1], d)
    for l in range(DEPTH):
        qs, qf, kb, kmean, vb, sgu = _projections(
            h, w_in[l].astype(BF16), sgu_ln_g[l], sgu_ln_b[l], w_s[l], b_s[l], tm=min(1024, s))
        attn = _moba(qs, qf, kb, kmean, vb, ck=4 * MOBA_BLOCK, hs=2)
        x2, x2t = _postmix(attn, sgu, h, mem2d, w_out[l], ln1_g[l], ln1_b[l], w_mq[l], w_mkv[l],
                           w_mo[l], ln2_g[l], ln2_b[l], tm=256)
        eidx_t, gw_t, counts = _router(x2, w_router[l], router_bias[l], tm=min(512, s))
        y = _routed_experts(x2t, eidx_t.T, counts, w1[l], w3[l], w2[l])
        h = _final(x2, gw_t.T, y, ws1[l], ws3[l], ws2[l], ln3_g[l], ln3_b[l], tm=128)
    return h.reshape(b, s, d)
```

```python
import functools

import jax
import jax.numpy as jnp
from jax import lax
from jax.experimental import pallas as pl
from jax.experimental.pallas import tpu as pltpu

F32 = jnp.float32
BF16 = jnp.bfloat16

N_ATTN_HEADS = 8
HEAD_DIM = 128
ATTN_WIDTH = N_ATTN_HEADS * HEAD_DIM
N_SGU_GROUPS = 8
SGU_GROUP_DIM = 128
SGU_WIDTH = N_SGU_GROUPS * SGU_GROUP_DIM
SGU_CHUNK = 128
MOBA_BLOCK = 256
MOBA_TOPK = 3
ROPE_THETA = 500000.0
ROPE_DIM = HEAD_DIM // 4
MEM_HEADS = 4
MEM_HEAD_DIM = 128
N_EXPERTS = 256
TOP_K = 8
N_EXPERT_GROUPS = 8
GROUP_SIZE = N_EXPERTS // N_EXPERT_GROUPS
TOPK_GROUPS = 4
ROUTED_SCALE = 2.5
MOE_BLOCK = 128
LN_EPS = 1e-5
DEPTH = 1
DEEPNORM_ALPHA = (2 * DEPTH) ** 0.25

LANES = 128
MASK_NEG = -1e30
LOG2_E = 1.4426950408889634
VMEM_LIMIT = 56 * 1024 * 1024

_NT = (((1,), (1,)), ((), ()))


def _params(n_axes, vmem=VMEM_LIMIT):
    return pltpu.CompilerParams(dimension_semantics=("arbitrary",) * n_axes,
                                vmem_limit_bytes=vmem)


def _layer_norm(x, g, b):
    mu = jnp.mean(x, axis=-1, keepdims=True)
    xc = x - mu
    var = jnp.mean(xc * xc, axis=-1, keepdims=True)
    return xc * lax.rsqrt(var + LN_EPS) * g + b


def _silu(x):
    return x * jax.nn.sigmoid(x)


def _store_token_tiles(ref, x):
    rows, n = x.shape[0], x.shape[1] // LANES
    for c in range(n):
        ref[pl.ds(c, rows, stride=n), :] = x[:, c * LANES:(c + 1) * LANES]


def _load_token_tiles(ref, rows, n):
    return jnp.concatenate([ref[pl.ds(c, rows, stride=n), :] for c in range(n)], axis=1)


def _rope(acc, cosf, sinf):
    half = ROPE_DIM // 2
    rows = acc.shape[0]
    lane = lax.broadcasted_iota(jnp.int32, (rows, LANES), 1)
    outs = []
    for c in range(acc.shape[1] // LANES):
        t = acc[:, c * LANES:(c + 1) * LANES]
        partner = jnp.where(lane < half, pltpu.roll(t, LANES - half, 1), pltpu.roll(t, half, 1))
        outs.append(t * cosf + partner * sinf)
    return jnp.concatenate(outs, axis=1)


def _q_kernel(x_ref, w_ref, cos_ref, sin_ref, qs_ref, qf_ref):
    acc = jnp.dot(x_ref[...].astype(BF16), w_ref[...], preferred_element_type=F32)
    r = _rope(acc, cos_ref[...], sin_ref[...])
    qf_ref[...] = r
    qs_ref[...] = (r * (HEAD_DIM ** -0.5 * LOG2_E)).astype(BF16)


def _k_kernel(x_ref, w_ref, cos_ref, sin_ref, k_ref, km_ref):
    acc = jnp.dot(x_ref[...].astype(BF16), w_ref[...], preferred_element_type=F32)
    r = _rope(acc, cos_ref[...], sin_ref[...])
    k_ref[...] = r.astype(BF16)
    rows, cols = r.shape
    km_ref[0] = jnp.mean(r.reshape(rows // MOBA_BLOCK, MOBA_BLOCK, cols), axis=1)


def _v_kernel(x_ref, w_ref, v_ref):
    v_ref[...] = jnp.dot(x_ref[...].astype(BF16), w_ref[...],
                         preferred_element_type=F32).astype(BF16)


def _sgu_kernel(x_ref, wu_ref, wz_ref, g_ref, b_ref, ws_ref, bs_ref, o_ref):
    xb = x_ref[...].astype(BF16)
    u = jax.nn.gelu(jnp.dot(xb, wu_ref[...], preferred_element_type=F32))
    z = jax.nn.gelu(jnp.dot(xb, wz_ref[...], preferred_element_type=F32))
    rows, cols = u.shape
    t_row = lax.broadcasted_iota(jnp.int32, (SGU_CHUNK, SGU_CHUNK), 0)
    s_col = lax.broadcasted_iota(jnp.int32, (SGU_CHUNK, SGU_CHUNK), 1)
    causal = s_col <= t_row
    for gl in range(cols // SGU_GROUP_DIM):
        sl = slice(gl * SGU_GROUP_DIM, (gl + 1) * SGU_GROUP_DIM)
        zn = _layer_norm(z[:, sl], g_ref[:, sl], b_ref[:, sl]).astype(BF16)
        w = jnp.where(causal, ws_ref[gl], 0.0).astype(BF16)
        bias = bs_ref[:, gl:gl + 1]
        for c in range(rows // SGU_CHUNK):
            rs = slice(c * SGU_CHUNK, (c + 1) * SGU_CHUNK)
            mixed = jnp.dot(w, zn[rs], preferred_element_type=F32) + bias
            o_ref[rs, sl] = (u[rs, sl] * mixed).astype(BF16)


def _projections(x2d, w_in_bf, sgu_ln_g, sgu_ln_b, w_s, b_s, tm):
    s, d = x2d.shape
    tn = 512
    nrow = s // tm
    half = ROPE_DIM // 2

    pos = jnp.arange(s)
    inv_freq = ROPE_THETA ** (-jnp.arange(half, dtype=F32) / half)
    ang = pos.astype(F32)[:, None] * inv_freq[None, :]
    cos, sin = jnp.cos(ang), jnp.sin(ang)
    rest = LANES - ROPE_DIM
    cosf = jnp.concatenate([cos, cos, jnp.ones((s, rest), F32)], axis=1)
    sinf = jnp.concatenate([-sin, sin, jnp.zeros((s, rest), F32)], axis=1)

    x_spec = pl.BlockSpec((tm, d), lambda i, j: (i, 0))
    tab_spec = pl.BlockSpec((tm, LANES), lambda i, j: (i, 0))
    out_spec = pl.BlockSpec((tm, tn), lambda i, j: (i, j))
    nq = ATTN_WIDTH // tn

    def w_spec(col0):
        return pl.BlockSpec((d, tn), lambda i, j: (0, col0 // tn + j))

    qs, qf = pl.pallas_call(
        _q_kernel, grid=(nrow, nq),
        in_specs=[x_spec, w_spec(0), tab_spec, tab_spec],
        out_specs=[out_spec, out_spec],
        out_shape=[jax.ShapeDtypeStruct((s, ATTN_WIDTH), BF16),
                   jax.ShapeDtypeStruct((s, ATTN_WIDTH), F32)],
        compiler_params=_params(2), name="proj_q",
    )(x2d, w_in_bf, cosf, sinf)

    kpb = tm // MOBA_BLOCK
    kb, kmean = pl.pallas_call(
        _k_kernel, grid=(nrow, nq),
        in_specs=[x_spec, w_spec(ATTN_WIDTH), tab_spec, tab_spec],
        out_specs=[out_spec, pl.BlockSpec((1, kpb, tn), lambda i, j: (i, 0, j))],
        out_shape=[jax.ShapeDtypeStruct((s, ATTN_WIDTH), BF16),
                   jax.ShapeDtypeStruct((nrow, kpb, ATTN_WIDTH), F32)],
        compiler_params=_params(2), name="proj_k",
    )(x2d, w_in_bf, cosf, sinf)
    kmean = kmean.reshape(s // MOBA_BLOCK, ATTN_WIDTH)

    vb = pl.pallas_call(
        _v_kernel, grid=(nrow, nq),
        in_specs=[x_spec, w_spec(2 * ATTN_WIDTH)],
        out_specs=out_spec,
        out_shape=jax.ShapeDtypeStruct((s, ATTN_WIDTH), BF16),
        compiler_params=_params(2), name="proj_v",
    )(x2d, w_in_bf)

    gpb = tn // SGU_GROUP_DIM
    sgu = pl.pallas_call(
        _sgu_kernel, grid=(nrow, SGU_WIDTH // tn),
        in_specs=[x_spec, w_spec(3 * ATTN_WIDTH), w_spec(3 * ATTN_WIDTH + SGU_WIDTH),
                  pl.BlockSpec((1, tn), lambda i, j: (0, j)),
                  pl.BlockSpec((1, tn), lambda i, j: (0, j)),
                  pl.BlockSpec((gpb, SGU_CHUNK, SGU_CHUNK), lambda i, j: (j, 0, 0)),
                  pl.BlockSpec((None, SGU_CHUNK, gpb), lambda i, j: (j, 0, 0))],
        out_specs=out_spec,
        out_shape=jax.ShapeDtypeStruct((s, SGU_WIDTH), BF16),
        compiler_params=_params(2), name="proj_sgu",
    )(x2d, w_in_bf, w_in_bf, sgu_ln_g.reshape(1, SGU_WIDTH), sgu_ln_b.reshape(1, SGU_WIDTH),
      w_s, b_s.reshape(N_SGU_GROUPS // gpb, gpb, SGU_CHUNK).transpose(0, 2, 1))
    return qs, qf, kb, kmean, vb, sgu


def _moba_kernel(qs_ref, qf_ref, km_ref, oh_ref, k_ref, v_ref, o_ref,
                 qaug_scr, sa_scr, sb_scr, m_scr, acc_scr, *, tq, ck, hs):
    i = pl.program_id(1)
    own = i
    n_full = own // (ck // MOBA_BLOCK)
    col = lax.broadcasted_iota(jnp.int32, (tq, LANES), 1)
    colf = col.astype(F32)

    for hh in range(hs):
        hsl = slice(hh * HEAD_DIM, (hh + 1) * HEAD_DIM)
        gate = lax.dot_general(qf_ref[:, hsl], km_ref[:, hsl], _NT,
                               precision=lax.Precision.HIGHEST,
                               preferred_element_type=F32)
        gate = jnp.where(col < own, gate, -jnp.inf)
        selbias = jnp.where(col == own, 0.0, MASK_NEG)
        for _ in range(MOBA_TOPK):
            gmax = jnp.max(gate, axis=-1, keepdims=True)
            cand = jnp.where(gate == gmax, colf, float(LANES))
            cand = jnp.where(gmax > -jnp.inf, cand, float(LANES))
            first = jnp.min(cand, axis=-1, keepdims=True)
            pick = colf == first
            selbias = jnp.where(pick, 0.0, selbias)
            gate = jnp.where(pick, -jnp.inf, gate)
        qaug_scr[hh] = jnp.concatenate([qs_ref[:, hsl], selbias.astype(BF16)], axis=1)

    def scores(c0, s_ref, causal=None):
        c0 = pl.multiple_of(c0, ck)
        onehot = oh_ref[pl.ds(c0, ck), :]
        for hh in range(hs):
            hsl = slice(hh * HEAD_DIM, (hh + 1) * HEAD_DIM)
            kaug = jnp.concatenate([k_ref[pl.ds(c0, ck), hsl], onehot], axis=1)
            sc = lax.dot_general(qaug_scr[hh], kaug, _NT, preferred_element_type=F32)
            if causal is not None:
                sc = jnp.where(causal, sc, MASK_NEG)
            s_ref[hh] = sc

    ones = jnp.ones((ck, HEAD_DIM), BF16)

    def absorb(c0, s_ref):
        c0 = pl.multiple_of(c0, ck)
        for hh in range(hs):
            hsl = slice(hh * HEAD_DIM, (hh + 1) * HEAD_DIM)
            sc = s_ref[hh]
            m_old = m_scr[hh]
            m_new = jnp.maximum(m_old, jnp.max(sc, axis=-1, keepdims=True))
            alpha = jnp.exp2(m_old - m_new)
            pn = jnp.exp2(sc - m_new[:, :1])
            v1 = jnp.concatenate([v_ref[pl.ds(c0, ck), hsl], ones], axis=1)
            acc_scr[hh] = jnp.concatenate([alpha, alpha], axis=1) * acc_scr[hh] + jnp.dot(
                pn.astype(BF16), v1, preferred_element_type=F32)
            m_scr[hh] = m_new

    m_scr[...] = jnp.full(m_scr.shape, MASK_NEG, F32)
    acc_scr[...] = jnp.zeros(acc_scr.shape, F32)
    d0 = n_full * ck
    q_pos = i * tq + lax.broadcasted_iota(jnp.int32, (tq, ck), 0)
    k_pos = d0 + lax.broadcasted_iota(jnp.int32, (tq, ck), 1)
    scores(d0, sa_scr, k_pos <= q_pos)

    def pair(u, carry):
        prev0 = jnp.where(u == 0, d0, (2 * u - 1) * ck)
        absorb(prev0, sa_scr)
        scores(2 * u * ck, sb_scr)
        absorb(2 * u * ck, sb_scr)
        scores((2 * u + 1) * ck, sa_scr)
        return carry

    n_pairs = n_full // 2
    lax.fori_loop(0, n_pairs, pair, 0)
    last0 = jnp.where(n_pairs == 0, d0, (2 * n_pairs - 1) * ck)

    @pl.when(n_full % 2 == 1)
    def _():
        absorb(last0, sa_scr)
        scores((n_full - 1) * ck, sb_scr)
        absorb((n_full - 1) * ck, sb_scr)

    @pl.when(n_full % 2 == 0)
    def _():
        absorb(last0, sa_scr)

    for hh in range(hs):
        acc = acc_scr[hh]
        o_ref[:, hh * HEAD_DIM:(hh + 1) * HEAD_DIM] = (
            acc[:, :HEAD_DIM] / acc[:, HEAD_DIM:]).astype(BF16)


def _moba(qs, qf, kb, kmean, vb, ck, hs):
    s = qs.shape[0]
    tq = MOBA_BLOCK
    nb = s // MOBA_BLOCK
    assert nb <= LANES and s % ck == 0 and ck % MOBA_BLOCK == 0 and N_ATTN_HEADS % hs == 0
    km_pad = jnp.zeros((LANES, ATTN_WIDTH), F32).at[:nb].set(kmean)
    onehot = (jnp.arange(s)[:, None] // MOBA_BLOCK == jnp.arange(LANES)[None, :]).astype(BF16)
    gw = hs * HEAD_DIM
    tile = pl.BlockSpec((tq, gw), lambda g, i: (i, g))
    once = pl.Buffered(1)
    return pl.pallas_call(
        functools.partial(_moba_kernel, tq=tq, ck=ck, hs=hs),
        grid=(N_ATTN_HEADS // hs, s // tq),
        in_specs=[tile, tile,
                  pl.BlockSpec((LANES, gw), lambda g, i: (0, g)),
                  pl.BlockSpec((s, LANES), lambda g, i: (0, 0), pipeline_mode=once),
                  pl.BlockSpec((s, gw), lambda g, i: (0, g), pipeline_mode=once),
                  pl.BlockSpec((s, gw), lambda g, i: (0, g), pipeline_mode=once)],
        out_specs=tile,
        out_shape=jax.ShapeDtypeStruct((s, ATTN_WIDTH), BF16),
        scratch_shapes=[pltpu.VMEM((hs, tq, 2 * HEAD_DIM), BF16)] +
                       [pltpu.VMEM((hs, tq, ck), F32)] * 2 +
                       [pltpu.VMEM((hs, tq, LANES), F32),
                        pltpu.VMEM((hs, tq, 2 * HEAD_DIM), F32)],
        compiler_params=_params(2), name="moba",
    )(qs, qf, km_pad, onehot, kb, vb)


def _memkv_kernel(mem_ref, w_ref, kv_ref):
    kv_ref[...] = jnp.dot(mem_ref[...].astype(BF16), w_ref[...],
                          preferred_element_type=F32).astype(BF16)


def _postmix_kernel(attn_ref, sgu_ref, x_ref, wout_ref, g1_ref, b1_ref, wmq_ref, kv_ref,
                    wmo_ref, g2_ref, b2_ref, o_ref, op_ref):
    mixin = jnp.concatenate([attn_ref[...], sgu_ref[...]], axis=1)
    mix = jnp.dot(mixin, wout_ref[...], preferred_element_type=F32)
    y1 = _layer_norm(DEEPNORM_ALPHA * x_ref[...] + mix, g1_ref[...], b1_ref[...])

    q = jnp.dot(y1.astype(BF16), wmq_ref[...], preferred_element_type=F32)
    q = (q * (MEM_HEAD_DIM ** -0.5)).astype(BF16)
    width = MEM_HEADS * MEM_HEAD_DIM
    heads = []
    for h in range(MEM_HEADS):
        sl = slice(h * MEM_HEAD_DIM, (h + 1) * MEM_HEAD_DIM)
        sc = lax.dot_general(q[:, sl], kv_ref[:, sl], _NT, preferred_element_type=F32)
        p = jnp.exp(sc - jnp.max(sc, axis=-1, keepdims=True))
        denom = jnp.sum(p, axis=-1, keepdims=True)
        vsl = slice(width + h * MEM_HEAD_DIM, width + (h + 1) * MEM_HEAD_DIM)
        heads.append(jnp.dot(p.astype(BF16), kv_ref[:, vsl], preferred_element_type=F32) / denom)
    o = jnp.concatenate(heads, axis=1).astype(BF16)
    xat = jnp.dot(o, wmo_ref[...], preferred_element_type=F32)
    y2 = _layer_norm(DEEPNORM_ALPHA * y1 + xat, g2_ref[...], b2_ref[...])
    o_ref[...] = y2
    _store_token_tiles(op_ref, y2)


def _postmix(attn, sgu, x2d, mem2d, w_out, ln1_g, ln1_b, w_mq, w_mkv, w_mo, ln2_g, ln2_b, tm):
    s, d = x2d.shape
    m = mem2d.shape[0]
    width = MEM_HEADS * MEM_HEAD_DIM
    kv = pl.pallas_call(
        _memkv_kernel,
        out_shape=jax.ShapeDtypeStruct((m, 2 * width), BF16),
        compiler_params=_params(0), name="mem_kv",
    )(mem2d, w_mkv.astype(BF16))

    def full(shape):
        return pl.BlockSpec(shape, lambda i: (0,) * len(shape))

    def rows(cols):
        return pl.BlockSpec((tm, cols), lambda i: (i, 0))

    return pl.pallas_call(
        _postmix_kernel, grid=(s // tm,),
        in_specs=[rows(ATTN_WIDTH), rows(SGU_WIDTH), rows(d), full((ATTN_WIDTH + SGU_WIDTH, d)),
                  full((1, d)), full((1, d)), full((d, width)), full((m, 2 * width)),
                  full((width, d)), full((1, d)), full((1, d))],
        out_specs=[rows(d), pl.BlockSpec((tm * (d // LANES), LANES), lambda i: (i, 0))],
        out_shape=[jax.ShapeDtypeStruct((s, d), F32),
                   jax.ShapeDtypeStruct((s * (d // LANES), LANES), F32)],
        compiler_params=_params(1), name="postmix",
    )(attn, sgu, x2d, w_out.astype(BF16), ln1_g.reshape(1, d), ln1_b.reshape(1, d),
      w_mq.astype(BF16), kv, w_mo.astype(BF16), ln2_g.reshape(1, d), ln2_b.reshape(1, d))


def _router_kernel(x_ref, wh_ref, wl_ref, bias_ref, eidx_ref, gw_ref, cnt_ref):
    tm = x_ref.shape[0]
    x = x_ref[...]
    xh = x.astype(BF16)
    xl = (x - xh.astype(F32)).astype(BF16)
    wh = wh_ref[...]
    logits = (lax.dot_general(wh, xh, _NT, preferred_element_type=F32)
              + lax.dot_general(wh, xl, _NT, preferred_element_type=F32)
              + lax.dot_general(wl_ref[...], xh, _NT, preferred_element_type=F32))
    scores = jax.nn.sigmoid(logits)
    biased = scores + bias_ref[...]

    g3 = biased.reshape(N_EXPERT_GROUPS, GROUP_SIZE, tm)
    in_grp = lax.broadcasted_iota(jnp.int32, g3.shape, 1).astype(F32)
    top1 = jnp.max(g3, axis=1, keepdims=True)
    first = jnp.min(jnp.where(g3 == top1, in_grp, float(GROUP_SIZE)), axis=1, keepdims=True)
    top2 = jnp.max(jnp.where(in_grp == first, -jnp.inf, g3), axis=1, keepdims=True)
    grp_score = (top1 + top2).reshape(N_EXPERT_GROUPS, tm)

    gid = lax.broadcasted_iota(jnp.int32, grp_score.shape, 0).astype(F32)
    grp_keep = jnp.zeros(grp_score.shape, F32)
    for _ in range(TOPK_GROUPS):
        best = jnp.max(grp_score, axis=0, keepdims=True)
        first = jnp.min(jnp.where(grp_score == best, gid, float(N_EXPERT_GROUPS)),
                        axis=0, keepdims=True)
        pick = gid == first
        grp_keep = jnp.where(pick, 1.0, grp_keep)
        grp_score = jnp.where(pick, -jnp.inf, grp_score)
    keep = jnp.broadcast_to(grp_keep.reshape(N_EXPERT_GROUPS, 1, tm), g3.shape) > 0.5
    masked = jnp.where(keep, g3, -jnp.inf).reshape(N_EXPERTS, tm)

    eid = lax.broadcasted_iota(jnp.int32, masked.shape, 0).astype(F32)
    idx_rows, w_rows = [], []
    chosen = jnp.zeros(masked.shape, F32)
    for _ in range(TOP_K):
        best = jnp.max(masked, axis=0, keepdims=True)
        first = jnp.min(jnp.where(masked == best, eid, float(N_EXPERTS)), axis=0, keepdims=True)
        pick = eid == first
        idx_rows.append(first)
        w_rows.append(jnp.sum(jnp.where(pick, scores, 0.0), axis=0, keepdims=True))
        masked = jnp.where(pick, -jnp.inf, masked)
        chosen = jnp.where(pick, 1.0, chosen)
    gw = jnp.concatenate(w_rows, axis=0)
    gw = gw / jnp.sum(gw, axis=0, keepdims=True) * ROUTED_SCALE
    eidx_ref[...] = jnp.concatenate(idx_rows, axis=0).astype(jnp.int32)
    gw_ref[...] = gw

    @pl.when(pl.program_id(0) == 0)
    def _():
        cnt_ref[...] = jnp.zeros(cnt_ref.shape, F32)

    cnt_ref[...] += jnp.sum(chosen, axis=1, keepdims=True)


def _router(x2, w_router, router_bias, tm):
    t, d = x2.shape
    wr_t = w_router.T.astype(F32)
    wr_hi = wr_t.astype(BF16)
    wr_lo = (wr_t - wr_hi.astype(F32)).astype(BF16)
    return pl.pallas_call(
        _router_kernel, grid=(t // tm,),
        in_specs=[pl.BlockSpec((tm, d), lambda i: (i, 0)),
                  pl.BlockSpec((N_EXPERTS, d), lambda i: (0, 0)),
                  pl.BlockSpec((N_EXPERTS, d), lambda i: (0, 0)),
                  pl.BlockSpec((N_EXPERTS, 1), lambda i: (0, 0))],
        out_specs=[pl.BlockSpec((TOP_K, tm), lambda i: (0, i)),
                   pl.BlockSpec((TOP_K, tm), lambda i: (0, i)),
                   pl.BlockSpec((N_EXPERTS, 1), lambda i: (0, 0))],
        out_shape=[jax.ShapeDtypeStruct((TOP_K, t), jnp.int32),
                   jax.ShapeDtypeStruct((TOP_K, t), F32),
                   jax.ShapeDtypeStruct((N_EXPERTS, 1), F32)],
        compiler_params=_params(1), name="router",
    )(x2, wr_hi, wr_lo, router_bias.reshape(N_EXPERTS, 1).astype(F32))


IDS_WINDOW = 2048
IDS_ALIGN = 1024
X_SLOTS, Y_SLOTS, ID_SLOTS = 4, 2, 8
DMA_GROUPS = 8


def _expert_kernel(bstart_ref, bcount_ref, jwin_ref, joff_ref, nvalid_ref, nused_ref,
                   ids_hbm, x_hbm, w1_ref, w3_ref, w2_ref, y_hbm,
                   ids_smem, xbuf, ybuf, isem, gsem, ssem, *, n_sorted):
    e = pl.program_id(0)
    nused = nused_ref[0]
    rows = MOE_BLOCK
    tpr = w1_ref.shape[0] // LANES
    spare0 = y_hbm.shape[0] - Y_SLOTS * rows

    def x_row(tok):
        return x_hbm.at[pl.ds(pl.multiple_of(tok * tpr, tpr), tpr)]

    def ids_copies(blk, slot):
        src0 = pl.multiple_of(jwin_ref[blk] * IDS_ALIGN, IDS_ALIGN)
        dst0 = pl.multiple_of(slot * 2 * IDS_WINDOW, IDS_WINDOW)
        return [pltpu.make_async_copy(ids_hbm.at[pl.ds(src0 + part * n_sorted, IDS_WINDOW)],
                                      ids_smem.at[pl.ds(dst0 + part * IDS_WINDOW, IDS_WINDOW)],
                                      isem.at[slot]) for part in range(2)]

    def start_ids(blk, slot):
        for c in ids_copies(blk, slot):
            c.start()

    def wait_ids(blk):
        for c in ids_copies(blk, blk % ID_SLOTS):
            c.wait()

    def gather_rows(blk, slot, r0, r1):
        base = (blk % ID_SLOTS) * 2 * IDS_WINDOW + joff_ref[blk]
        for r in range(r0, r1):
            tok = ids_smem[base + r]
            pltpu.make_async_copy(x_row(tok), xbuf.at[slot, pl.ds(r * tpr, tpr)],
                                  gsem.at[slot]).start(priority=r % 2)

    def wait_gather(blk):
        slot = blk % X_SLOTS
        pltpu.make_async_copy(x_hbm.at[pl.ds(0, rows * tpr)], xbuf.at[slot],
                              gsem.at[slot]).wait()

    def scatter_rows(blk, nvalid, slot, r0, r1):
        base = (blk % ID_SLOTS) * 2 * IDS_WINDOW + IDS_WINDOW + joff_ref[blk]
        spare = spare0 + slot * rows
        for r in range(r0, r1):
            dst = jnp.where(r < nvalid, ids_smem[base + r], spare + r)
            pltpu.make_async_copy(ybuf.at[slot, pl.ds(r, 1)], y_hbm.at[pl.ds(dst, 1)],
                                  ssem.at[slot]).start(priority=r % 2)

    def wait_scatter(slot):
        pltpu.make_async_copy(ybuf.at[slot], y_hbm.at[pl.ds(0, rows)], ssem.at[slot]).wait()

    def block_body(phase, g):
        ys, ps, ahead = phase % Y_SLOTS, (phase + 1) % Y_SLOTS, (phase + 2) % X_SLOTS
        prev = jnp.maximum(g - 1, 0)
        nvalid_prev = jnp.where(g > 0, nvalid_ref[prev], 0)
        per = rows // DMA_GROUPS
        groups = iter(range(DMA_GROUPS))

        def dma_group():
            k = next(groups)
            gather_rows(g + 2, ahead, k * per, (k + 1) * per)
            scatter_rows(prev, nvalid_prev, ps, k * per, (k + 1) * per)

        xb = _load_token_tiles(xbuf.at[phase], rows, tpr)
        nh = w1_ref.shape[1] // 2
        parts = []
        for w_ref in (w1_ref, w3_ref):
            for c in range(2):
                parts.append(jnp.dot(xb, w_ref[:, c * nh:(c + 1) * nh],
                                     preferred_element_type=F32))
                dma_group()
        h = _silu(jnp.concatenate(parts[:2], axis=1)) * jnp.concatenate(parts[2:], axis=1)
        ny = w2_ref.shape[1] // 4
        for c in range(4):
            ybuf[ys, :, c * ny:(c + 1) * ny] = jnp.dot(h, w2_ref[:, c * ny:(c + 1) * ny],
                                                       preferred_element_type=F32)
            dma_group()
        return 0

    @pl.when(e == 0)
    def _():
        for blk in range(4):
            start_ids(blk, blk)
        ybuf[...] = jnp.zeros(ybuf.shape, F32)
        for s in range(Y_SLOTS):
            spare = pltpu.make_async_copy(ybuf.at[0], y_hbm.at[pl.ds(spare0 + s * rows, rows)],
                                          ssem.at[s])
            spare.start()
            spare.wait()
        for blk in range(2):
            wait_ids(blk)
            gather_rows(blk, blk, 0, rows)

    def block(j, carry):
        g = bstart_ref[e] + j
        start_ids(g + 4, (g + 4) % ID_SLOTS)
        wait_ids(g + 2)
        wait_gather(g)

        @pl.when(g >= 1)
        def _():
            wait_scatter(g % Y_SLOTS)

        lax.switch(g % X_SLOTS, [functools.partial(block_body, ph) for ph in range(X_SLOTS)], g)
        return carry

    lax.fori_loop(0, bcount_ref[e], block, 0)

    @pl.when(e == pl.num_programs(0) - 1)
    def _():
        last = nused - 1

        def scatter_last(slot, _):
            scatter_rows(last, nvalid_ref[last], slot, 0, rows)
            return 0

        lax.switch(last % Y_SLOTS, [functools.partial(scatter_last, s) for s in range(Y_SLOTS)], 0)
        for s in range(Y_SLOTS):
            wait_scatter(s)
        for blk in range(1, 3):
            wait_gather(last + blk)
        for blk in range(3, 5):
            wait_ids(last + blk)


def _routed_experts(xt, eidx, counts, w1, w3, w2):
    e, d, f = w1.shape
    tpr = d // LANES
    t = xt.shape[0] // tpr
    m = t * TOP_K
    flat_e = eidx.reshape(m)
    _, order = lax.sort_key_val(flat_e, jnp.arange(m, dtype=jnp.int32))
    tok_sorted = order // TOP_K
    dst_sorted = (order % TOP_K) * t + tok_sorted
    n_sorted = m + IDS_WINDOW
    pad = jnp.zeros((IDS_WINDOW,), jnp.int32)
    ids = jnp.concatenate([tok_sorted, pad, dst_sorted, pad])

    counts = counts.reshape(e).astype(jnp.int32)
    ends = jnp.cumsum(counts)
    starts = ends - counts
    pcounts = (counts + MOE_BLOCK - 1) // MOE_BLOCK * MOE_BLOCK
    pends = jnp.cumsum(pcounts)
    pstarts = pends - pcounts
    n_blk = -(-m // MOE_BLOCK) + e
    nused = (pends[-1] // MOE_BLOCK).astype(jnp.int32)
    n_meta = n_blk + ID_SLOTS
    row0 = jnp.arange(n_meta, dtype=jnp.int32) * MOE_BLOCK
    blk_e = jnp.minimum(jnp.sum(row0[:, None] >= pends[None, :], axis=1), e - 1).astype(jnp.int32)
    live = jnp.arange(n_meta) < nused
    onehot = blk_e[:, None] == jnp.arange(e)[None, :]
    pick = lambda v: jnp.sum(jnp.where(onehot, v[None, :], 0), axis=1)
    in_seg = row0 - pick(pstarts)
    jstart = jnp.where(live, pick(starts) + in_seg, 0)
    nvalid = jnp.where(live, jnp.clip(pick(counts) - in_seg, 0, MOE_BLOCK), 0).astype(jnp.int32)
    jwin = (jstart // IDS_ALIGN).astype(jnp.int32)
    joff = (jstart % IDS_ALIGN).astype(jnp.int32)
    bstart = (pstarts // MOE_BLOCK).astype(jnp.int32)
    bcount = (pcounts // MOE_BLOCK).astype(jnp.int32)

    grid_spec = pltpu.PrefetchScalarGridSpec(
        num_scalar_prefetch=6, grid=(e,),
        in_specs=[pl.BlockSpec(memory_space=pl.ANY),
                  pl.BlockSpec(memory_space=pl.ANY),
                  pl.BlockSpec((None, d, f), lambda i, *_: (i, 0, 0)),
                  pl.BlockSpec((None, d, f), lambda i, *_: (i, 0, 0)),
                  pl.BlockSpec((None, f, d), lambda i, *_: (i, 0, 0))],
        out_specs=pl.BlockSpec(memory_space=pl.ANY),
        scratch_shapes=[pltpu.SMEM((ID_SLOTS * 2 * IDS_WINDOW,), jnp.int32),
                        pltpu.VMEM((X_SLOTS, MOE_BLOCK * tpr, LANES), F32),
                        pltpu.VMEM((Y_SLOTS, MOE_BLOCK, d), F32),
                        pltpu.SemaphoreType.DMA((ID_SLOTS,)),
                        pltpu.SemaphoreType.DMA((X_SLOTS,)),
                        pltpu.SemaphoreType.DMA((Y_SLOTS,))])
    return pl.pallas_call(
        functools.partial(_expert_kernel, n_sorted=n_sorted), grid_spec=grid_spec,
        out_shape=jax.ShapeDtypeStruct((m + Y_SLOTS * MOE_BLOCK, d), F32),
        compiler_params=_params(1), name="experts",
    )(bstart, bcount, jwin, joff, nvalid, nused.reshape(1), ids, xt, w1, w3, w2)


def _final_kernel(x_ref, gw_ref, *refs):
    y_refs = refs[:TOP_K]
    ws1_ref, ws3_ref, ws2_ref, g_ref, b_ref, o_ref = refs[TOP_K:]
    x = x_ref[...]
    xb = x.astype(BF16)
    hid = _silu(jnp.dot(xb, ws1_ref[...], preferred_element_type=F32)) * jnp.dot(
        xb, ws3_ref[...], preferred_element_type=F32)
    shared = jnp.dot(hid.astype(BF16), ws2_ref[...], preferred_element_type=F32)
    gw = gw_ref[...]
    routed = gw[:, 0:1] * y_refs[0][...]
    for k in range(1, TOP_K):
        routed = routed + gw[:, k:k + 1] * y_refs[k][...]
    o_ref[...] = _layer_norm(DEEPNORM_ALPHA * x + (routed + shared), g_ref[...], b_ref[...])


def _final(x2, gw, y, ws1, ws3, ws2, ln3_g, ln3_b, tm):
    t, d = x2.shape
    f = ws1.shape[1]
    nrow = t // tm

    def full(shape):
        return pl.BlockSpec(shape, lambda i: (0,) * len(shape))

    y_specs = [pl.BlockSpec((tm, d), functools.partial(lambda i, k: (k * nrow + i, 0), k=k))
               for k in range(TOP_K)]
    return pl.pallas_call(
        _final_kernel, grid=(nrow,),
        in_specs=[pl.BlockSpec((tm, d), lambda i: (i, 0)),
                  pl.BlockSpec((tm, TOP_K), lambda i: (i, 0))] + y_specs +
                 [full((d, f)), full((d, f)), full((f, d)), full((1, d)), full((1, d))],
        out_specs=pl.BlockSpec((tm, d), lambda i: (i, 0)),
        out_shape=jax.ShapeDtypeStruct((t, d), F32),
        compiler_params=_params(1), name="final",
    )(x2, gw, *([y] * TOP_K), ws1.astype(BF16), ws3.astype(BF16), ws2.astype(BF16),
      ln3_g.reshape(1, d), ln3_b.reshape(1, d))


def kernel(x, mem, w_in, sgu_ln_g, sgu_ln_b, w_s, b_s, w_out, ln1_g, ln1_b, w_mq, w_mkv, w_mo,
           ln2_g, ln2_b, w_router, router_bias, w1, w3, w2, ws1, ws3, ws2, ln3_g, ln3_b):
    b, s, d = x.shape
    assert b == 1
    h = x.reshape(s, d)
    mem2d = mem.reshape(mem.shape[1], d)
    for l in range(DEPTH):
        qs, qf, kb, kmean, vb, sgu = _projections(
            h, w_in[l].astype(BF16), sgu_ln_g[l], sgu_ln_b[l], w_s[l], b_s[l], tm=min(1024, s))
        attn = _moba(qs, qf, kb, kmean, vb, ck=4 * MOBA_BLOCK, hs=4)
        x2, x2t = _postmix(attn, sgu, h, mem2d, w_out[l], ln1_g[l], ln1_b[l], w_mq[l], w_mkv[l],
                           w_mo[l], ln2_g[l], ln2_b[l], tm=256)
        eidx_t, gw_t, counts = _router(x2, w_router[l], router_bias[l], tm=min(512, s))
        y = _routed_experts(x2t, eidx_t.T, counts, w1[l], w3[l], w2[l])
        h = _final(x2, gw_t.T, y, ws1[l], ws3[l], ws2[l], ln3_g[l], ln3_b[l], tm=128)
    return h.reshape(b, s, d)
```

```python
import functools

import jax
import jax.numpy as jnp
from jax import lax
from jax.experimental import pallas as pl
from jax.experimental.pallas import tpu as pltpu

F32 = jnp.float32
BF16 = jnp.bfloat16

N_ATTN_HEADS = 8
HEAD_DIM = 128
ATTN_WIDTH = N_ATTN_HEADS * HEAD_DIM
N_SGU_GROUPS = 8
SGU_GROUP_DIM = 128
SGU_WIDTH = N_SGU_GROUPS * SGU_GROUP_DIM
SGU_CHUNK = 128
MOBA_BLOCK = 256
MOBA_TOPK = 3
ROPE_THETA = 500000.0
ROPE_DIM = HEAD_DIM // 4
MEM_HEADS = 4
MEM_HEAD_DIM = 128
N_EXPERTS = 256
TOP_K = 8
N_EXPERT_GROUPS = 8
GROUP_SIZE = N_EXPERTS // N_EXPERT_GROUPS
TOPK_GROUPS = 4
ROUTED_SCALE = 2.5
MOE_BLOCK = 128
LN_EPS = 1e-5
DEPTH = 1
DEEPNORM_ALPHA = (2 * DEPTH) ** 0.25

LANES = 128
MASK_NEG = -1e30
LOG2_E = 1.4426950408889634
VMEM_LIMIT = 56 * 1024 * 1024

_NT = (((1,), (1,)), ((), ()))


def _params(n_axes, vmem=VMEM_LIMIT):
    return pltpu.CompilerParams(dimension_semantics=("arbitrary",) * n_axes,
                                vmem_limit_bytes=vmem)


def _layer_norm(x, g, b):
    mu = jnp.mean(x, axis=-1, keepdims=True)
    xc = x - mu
    var = jnp.mean(xc * xc, axis=-1, keepdims=True)
    return xc * lax.rsqrt(var + LN_EPS) * g + b


def _silu(x):
    return x * jax.nn.sigmoid(x)


def _store_token_tiles(ref, x):
    rows, n = x.shape[0], x.shape[1] // LANES
    for c in range(n):
        ref[pl.ds(c, rows, stride=n), :] = x[:, c * LANES:(c + 1) * LANES]


def _load_token_tiles(ref, rows, n):
    return jnp.concatenate([ref[pl.ds(c, rows, stride=n), :] for c in range(n)], axis=1)


def _rope(acc, cosf, sinf):
    half = ROPE_DIM // 2
    rows = acc.shape[0]
    lane = lax.broadcasted_iota(jnp.int32, (rows, LANES), 1)
    outs = []
    for c in range(acc.shape[1] // LANES):
        t = acc[:, c * LANES:(c + 1) * LANES]
        partner = jnp.where(lane < half, pltpu.roll(t, LANES - half, 1), pltpu.roll(t, half, 1))
        outs.append(t * cosf + partner * sinf)
    return jnp.concatenate(outs, axis=1)


def _q_kernel(x_ref, w_ref, cos_ref, sin_ref, qs_ref, qf_ref):
    acc = jnp.dot(x_ref[...].astype(BF16), w_ref[...], preferred_element_type=F32)
    r = _rope(acc, cos_ref[...], sin_ref[...])
    qf_ref[...] = r
    qs_ref[...] = (r * (HEAD_DIM ** -0.5 * LOG2_E)).astype(BF16)


def _k_kernel(x_ref, w_ref, cos_ref, sin_ref, k_ref, km_ref):
    acc = jnp.dot(x_ref[...].astype(BF16), w_ref[...], preferred_element_type=F32)
    r = _rope(acc, cos_ref[...], sin_ref[...])
    k_ref[...] = r.astype(BF16)
    rows, cols = r.shape
    km_ref[0] = jnp.mean(r.reshape(rows // MOBA_BLOCK, MOBA_BLOCK, cols), axis=1)


def _v_kernel(x_ref, w_ref, v_ref):
    v_ref[...] = jnp.dot(x_ref[...].astype(BF16), w_ref[...],
                         preferred_element_type=F32).astype(BF16)


def _sgu_kernel(x_ref, wu_ref, wz_ref, g_ref, b_ref, ws_ref, bs_ref, o_ref):
    xb = x_ref[...].astype(BF16)
    u = jax.nn.gelu(jnp.dot(xb, wu_ref[...], preferred_element_type=F32))
    z = jax.nn.gelu(jnp.dot(xb, wz_ref[...], preferred_element_type=F32))
    rows, cols = u.shape
    t_row = lax.broadcasted_iota(jnp.int32, (SGU_CHUNK, SGU_CHUNK), 0)
    s_col = lax.broadcasted_iota(jnp.int32, (SGU_CHUNK, SGU_CHUNK), 1)
    causal = s_col <= t_row
    for gl in range(cols // SGU_GROUP_DIM):
        sl = slice(gl * SGU_GROUP_DIM, (gl + 1) * SGU_GROUP_DIM)
        zn = _layer_norm(z[:, sl], g_ref[:, sl], b_ref[:, sl]).astype(BF16)
        w = jnp.where(causal, ws_ref[gl], 0.0).astype(BF16)
        bias = bs_ref[:, gl:gl + 1]
        for c in range(rows // SGU_CHUNK):
            rs = slice(c * SGU_CHUNK, (c + 1) * SGU_CHUNK)
            mixed = jnp.dot(w, zn[rs], preferred_element_type=F32) + bias
            o_ref[rs, sl] = (u[rs, sl] * mixed).astype(BF16)


def _projections(x2d, w_in_bf, sgu_ln_g, sgu_ln_b, w_s, b_s, tm):
    s, d = x2d.shape
    tn = 512
    nrow = s // tm
    half = ROPE_DIM // 2

    pos = jnp.arange(s)
    inv_freq = ROPE_THETA ** (-jnp.arange(half, dtype=F32) / half)
    ang = pos.astype(F32)[:, None] * inv_freq[None, :]
    cos, sin = jnp.cos(ang), jnp.sin(ang)
    rest = LANES - ROPE_DIM
    cosf = jnp.concatenate([cos, cos, jnp.ones((s, rest), F32)], axis=1)
    sinf = jnp.concatenate([-sin, sin, jnp.zeros((s, rest), F32)], axis=1)

    x_spec = pl.BlockSpec((tm, d), lambda i, j: (i, 0))
    tab_spec = pl.BlockSpec((tm, LANES), lambda i, j: (i, 0))
    out_spec = pl.BlockSpec((tm, tn), lambda i, j: (i, j))
    nq = ATTN_WIDTH // tn

    def w_spec(col0):
        return pl.BlockSpec((d, tn), lambda i, j: (0, col0 // tn + j))

    qs, qf = pl.pallas_call(
        _q_kernel, grid=(nrow, nq),
        in_specs=[x_spec, w_spec(0), tab_spec, tab_spec],
        out_specs=[out_spec, out_spec],
        out_shape=[jax.ShapeDtypeStruct((s, ATTN_WIDTH), BF16),
                   jax.ShapeDtypeStruct((s, ATTN_WIDTH), F32)],
        compiler_params=_params(2), name="proj_q",
    )(x2d, w_in_bf, cosf, sinf)

    kpb = tm // MOBA_BLOCK
    kb, kmean = pl.pallas_call(
        _k_kernel, grid=(nrow, nq),
        in_specs=[x_spec, w_spec(ATTN_WIDTH), tab_spec, tab_spec],
        out_specs=[out_spec, pl.BlockSpec((1, kpb, tn), lambda i, j: (i, 0, j))],
        out_shape=[jax.ShapeDtypeStruct((s, ATTN_WIDTH), BF16),
                   jax.ShapeDtypeStruct((nrow, kpb, ATTN_WIDTH), F32)],
        compiler_params=_params(2), name="proj_k",
    )(x2d, w_in_bf, cosf, sinf)
    kmean = kmean.reshape(s // MOBA_BLOCK, ATTN_WIDTH)

    vb = pl.pallas_call(
        _v_kernel, grid=(nrow, nq),
        in_specs=[x_spec, w_spec(2 * ATTN_WIDTH)],
        out_specs=out_spec,
        out_shape=jax.ShapeDtypeStruct((s, ATTN_WIDTH), BF16),
        compiler_params=_params(2), name="proj_v",
    )(x2d, w_in_bf)

    gpb = tn // SGU_GROUP_DIM
    sgu = pl.pallas_call(
        _sgu_kernel, grid=(nrow, SGU_WIDTH // tn),
        in_specs=[x_spec, w_spec(3 * ATTN_WIDTH), w_spec(3 * ATTN_WIDTH + SGU_WIDTH),
                  pl.BlockSpec((1, tn), lambda i, j: (0, j)),
                  pl.BlockSpec((1, tn), lambda i, j: (0, j)),
                  pl.BlockSpec((gpb, SGU_CHUNK, SGU_CHUNK), lambda i, j: (j, 0, 0)),
                  pl.BlockSpec((None, SGU_CHUNK, gpb), lambda i, j: (j, 0, 0))],
        out_specs=out_spec,
        out_shape=jax.ShapeDtypeStruct((s, SGU_WIDTH), BF16),
        compiler_params=_params(2), name="proj_sgu",
    )(x2d, w_in_bf, w_in_bf, sgu_ln_g.reshape(1, SGU_WIDTH), sgu_ln_b.reshape(1, SGU_WIDTH),
      w_s, b_s.reshape(N_SGU_GROUPS // gpb, gpb, SGU_CHUNK).transpose(0, 2, 1))
    return qs, qf, kb, kmean, vb, sgu


def _moba_kernel(qs_ref, qf_ref, km_ref, oh_ref, k_ref, v_ref, o_ref,
                 qaug_scr, sa_scr, sb_scr, m_scr, acc_scr, *, tq, ck, hs):
    i = pl.program_id(1)
    own = i
    n_full = own // (ck // MOBA_BLOCK)
    col = lax.broadcasted_iota(jnp.int32, (tq, LANES), 1)
    colf = col.astype(F32)

    for hh in range(hs):
        hsl = slice(hh * HEAD_DIM, (hh + 1) * HEAD_DIM)
        gate = lax.dot_general(qf_ref[:, hsl], km_ref[:, hsl], _NT,
                               precision=lax.Precision.HIGHEST,
                               preferred_element_type=F32)
        gate = jnp.where(col < own, gate, -jnp.inf)
        selbias = jnp.where(col == own, 0.0, MASK_NEG)
        for _ in range(MOBA_TOPK):
            gmax = jnp.max(gate, axis=-1, keepdims=True)
            cand = jnp.where(gate == gmax, colf, float(LANES))
            cand = jnp.where(gmax > -jnp.inf, cand, float(LANES))
            first = jnp.min(cand, axis=-1, keepdims=True)
            pick = colf == first
            selbias = jnp.where(pick, 0.0, selbias)
            gate = jnp.where(pick, -jnp.inf, gate)
        qaug_scr[hh] = jnp.concatenate([qs_ref[:, hsl], selbias.astype(BF16)], axis=1)

    def scores(c0, s_ref, causal=None):
        c0 = pl.multiple_of(c0, ck)
        onehot = oh_ref[pl.ds(c0, ck), :]
        for hh in range(hs):
            hsl = slice(hh * HEAD_DIM, (hh + 1) * HEAD_DIM)
            kaug = jnp.concatenate([k_ref[pl.ds(c0, ck), hsl], onehot], axis=1)
            sc = lax.dot_general(qaug_scr[hh], kaug, _NT, preferred_element_type=F32)
            if causal is not None:
                sc = jnp.where(causal, sc, MASK_NEG)
            s_ref[hh] = sc

    ones = jnp.ones((ck, HEAD_DIM), BF16)

    def absorb(c0, s_ref):
        c0 = pl.multiple_of(c0, ck)
        for hh in range(hs):
            hsl = slice(hh * HEAD_DIM, (hh + 1) * HEAD_DIM)
            sc = s_ref[hh]
            m_old = m_scr[hh]
            m_new = jnp.maximum(m_old, jnp.max(sc, axis=-1, keepdims=True))
            alpha = jnp.exp2(m_old - m_new)
            pn = jnp.exp2(sc - m_new[:, :1])
            v1 = jnp.concatenate([v_ref[pl.ds(c0, ck), hsl], ones], axis=1)
            acc_scr[hh] = jnp.concatenate([alpha, alpha], axis=1) * acc_scr[hh] + jnp.dot(
                pn.astype(BF16), v1, preferred_element_type=F32)
            m_scr[hh] = m_new

    m_scr[...] = jnp.full(m_scr.shape, MASK_NEG, F32)
    acc_scr[...] = jnp.zeros(acc_scr.shape, F32)
    d0 = n_full * ck
    q_pos = i * tq + lax.broadcasted_iota(jnp.int32, (tq, ck), 0)
    k_pos = d0 + lax.broadcasted_iota(jnp.int32, (tq, ck), 1)
    scores(d0, sa_scr, k_pos <= q_pos)

    def pair(u, carry):
        prev0 = jnp.where(u == 0, d0, (2 * u - 1) * ck)
        absorb(prev0, sa_scr)
        scores(2 * u * ck, sb_scr)
        absorb(2 * u * ck, sb_scr)
        scores((2 * u + 1) * ck, sa_scr)
        return carry

    n_pairs = n_full // 2
    lax.fori_loop(0, n_pairs, pair, 0)
    last0 = jnp.where(n_pairs == 0, d0, (2 * n_pairs - 1) * ck)

    @pl.when(n_full % 2 == 1)
    def _():
        absorb(last0, sa_scr)
        scores((n_full - 1) * ck, sb_scr)
        absorb((n_full - 1) * ck, sb_scr)

    @pl.when(n_full % 2 == 0)
    def _():
        absorb(last0, sa_scr)

    for hh in range(hs):
        acc = acc_scr[hh]
        o_ref[:, hh * HEAD_DIM:(hh + 1) * HEAD_DIM] = (
            acc[:, :HEAD_DIM] / acc[:, HEAD_DIM:]).astype(BF16)


def _moba(qs, qf, kb, kmean, vb, ck, hs):
    s = qs.shape[0]
    tq = MOBA_BLOCK
    nb = s // MOBA_BLOCK
    assert nb <= LANES and s % ck == 0 and ck % MOBA_BLOCK == 0 and N_ATTN_HEADS % hs == 0
    km_pad = jnp.zeros((LANES, ATTN_WIDTH), F32).at[:nb].set(kmean)
    onehot = (jnp.arange(s)[:, None] // MOBA_BLOCK == jnp.arange(LANES)[None, :]).astype(BF16)
    gw = hs * HEAD_DIM
    tile = pl.BlockSpec((tq, gw), lambda g, i: (i, g))
    once = pl.Buffered(1)
    return pl.pallas_call(
        functools.partial(_moba_kernel, tq=tq, ck=ck, hs=hs),
        grid=(N_ATTN_HEADS // hs, s // tq),
        in_specs=[tile, tile,
                  pl.BlockSpec((LANES, gw), lambda g, i: (0, g)),
                  pl.BlockSpec((s, LANES), lambda g, i: (0, 0), pipeline_mode=once),
                  pl.BlockSpec((s, gw), lambda g, i: (0, g), pipeline_mode=once),
                  pl.BlockSpec((s, gw), lambda g, i: (0, g), pipeline_mode=once)],
        out_specs=tile,
        out_shape=jax.ShapeDtypeStruct((s, ATTN_WIDTH), BF16),
        scratch_shapes=[pltpu.VMEM((hs, tq, 2 * HEAD_DIM), BF16)] +
                       [pltpu.VMEM((hs, tq, ck), F32)] * 2 +
                       [pltpu.VMEM((hs, tq, LANES), F32),
                        pltpu.VMEM((hs, tq, 2 * HEAD_DIM), F32)],
        compiler_params=_params(2), name="moba",
    )(qs, qf, km_pad, onehot, kb, vb)


def _memkv_kernel(mem_ref, w_ref, kv_ref):
    kv_ref[...] = jnp.dot(mem_ref[...].astype(BF16), w_ref[...],
                          preferred_element_type=F32).astype(BF16)


def _postmix_kernel(attn_ref, sgu_ref, x_ref, wout_ref, g1_ref, b1_ref, wmq_ref, kv_ref,
                    wmo_ref, g2_ref, b2_ref, o_ref, op_ref):
    mixin = jnp.concatenate([attn_ref[...], sgu_ref[...]], axis=1)
    mix = jnp.dot(mixin, wout_ref[...], preferred_element_type=F32)
    y1 = _layer_norm(DEEPNORM_ALPHA * x_ref[...] + mix, g1_ref[...], b1_ref[...])

    q = jnp.dot(y1.astype(BF16), wmq_ref[...], preferred_element_type=F32)
    q = (q * (MEM_HEAD_DIM ** -0.5)).astype(BF16)
    width = MEM_HEADS * MEM_HEAD_DIM
    heads = []
    for h in range(MEM_HEADS):
        sl = slice(h * MEM_HEAD_DIM, (h + 1) * MEM_HEAD_DIM)
        sc = lax.dot_general(q[:, sl], kv_ref[:, sl], _NT, preferred_element_type=F32)
        p = jnp.exp(sc - jnp.max(sc, axis=-1, keepdims=True))
        denom = jnp.sum(p, axis=-1, keepdims=True)
        vsl = slice(width + h * MEM_HEAD_DIM, width + (h + 1) * MEM_HEAD_DIM)
        heads.append(jnp.dot(p.astype(BF16), kv_ref[:, vsl], preferred_element_type=F32) / denom)
    o = jnp.concatenate(heads, axis=1).astype(BF16)
    xat = jnp.dot(o, wmo_ref[...], preferred_element_type=F32)
    y2 = _layer_norm(DEEPNORM_ALPHA * y1 + xat, g2_ref[...], b2_ref[...])
    o_ref[...] = y2
    _store_token_tiles(op_ref, y2)


def _postmix(attn, sgu, x2d, mem2d, w_out, ln1_g, ln1_b, w_mq, w_mkv, w_mo, ln2_g, ln2_b, tm):
    s, d = x2d.shape
    m = mem2d.shape[0]
    width = MEM_HEADS * MEM_HEAD_DIM
    kv = pl.pallas_call(
        _memkv_kernel,
        out_shape=jax.ShapeDtypeStruct((m, 2 * width), BF16),
        compiler_params=_params(0), name="mem_kv",
    )(mem2d, w_mkv.astype(BF16))

    def full(shape):
        return pl.BlockSpec(shape, lambda i: (0,) * len(shape))

    def rows(cols):
        return pl.BlockSpec((tm, cols), lambda i: (i, 0))

    return pl.pallas_call(
        _postmix_kernel, grid=(s // tm,),
        in_specs=[rows(ATTN_WIDTH), rows(SGU_WIDTH), rows(d), full((ATTN_WIDTH + SGU_WIDTH, d)),
                  full((1, d)), full((1, d)), full((d, width)), full((m, 2 * width)),
                  full((width, d)), full((1, d)), full((1, d))],
        out_specs=[rows(d), pl.BlockSpec((tm * (d // LANES), LANES), lambda i: (i, 0))],
        out_shape=[jax.ShapeDtypeStruct((s, d), F32),
                   jax.ShapeDtypeStruct((s * (d // LANES), LANES), F32)],
        compiler_params=_params(1), name="postmix",
    )(attn, sgu, x2d, w_out.astype(BF16), ln1_g.reshape(1, d), ln1_b.reshape(1, d),
      w_mq.astype(BF16), kv, w_mo.astype(BF16), ln2_g.reshape(1, d), ln2_b.reshape(1, d))


def _router_kernel(x_ref, wh_ref, wl_ref, bias_ref, eidx_ref, gw_ref, cnt_ref):
    tm = x_ref.shape[0]
    x = x_ref[...]
    xh = x.astype(BF16)
    xl = (x - xh.astype(F32)).astype(BF16)
    wh = wh_ref[...]
    logits = (lax.dot_general(wh, xh, _NT, preferred_element_type=F32)
              + lax.dot_general(wh, xl, _NT, preferred_element_type=F32)
              + lax.dot_general(wl_ref[...], xh, _NT, preferred_element_type=F32))
    scores = jax.nn.sigmoid(logits)
    biased = scores + bias_ref[...]

    g3 = biased.reshape(N_EXPERT_GROUPS, GROUP_SIZE, tm)
    in_grp = lax.broadcasted_iota(jnp.int32, g3.shape, 1).astype(F32)
    top1 = jnp.max(g3, axis=1, keepdims=True)
    first = jnp.min(jnp.where(g3 == top1, in_grp, float(GROUP_SIZE)), axis=1, keepdims=True)
    top2 = jnp.max(jnp.where(in_grp == first, -jnp.inf, g3), axis=1, keepdims=True)
    grp_score = (top1 + top2).reshape(N_EXPERT_GROUPS, tm)

    gid = lax.broadcasted_iota(jnp.int32, grp_score.shape, 0).astype(F32)
    grp_keep = jnp.zeros(grp_score.shape, F32)
    for _ in range(TOPK_GROUPS):
        best = jnp.max(grp_score, axis=0, keepdims=True)
        first = jnp.min(jnp.where(grp_score == best, gid, float(N_EXPERT_GROUPS)),
                        axis=0, keepdims=True)
        pick = gid == first
        grp_keep = jnp.where(pick, 1.0, grp_keep)
        grp_score = jnp.where(pick, -jnp.inf, grp_score)
    keep = jnp.broadcast_to(grp_keep.reshape(N_EXPERT_GROUPS, 1, tm), g3.shape) > 0.5
    masked = jnp.where(keep, g3, -jnp.inf).reshape(N_EXPERTS, tm)

    eid = lax.broadcasted_iota(jnp.int32, masked.shape, 0).astype(F32)
    idx_rows, w_rows = [], []
    chosen = jnp.zeros(masked.shape, F32)
    for _ in range(TOP_K):
        best = jnp.max(masked, axis=0, keepdims=True)
        first = jnp.min(jnp.where(masked == best, eid, float(N_EXPERTS)), axis=0, keepdims=True)
        pick = eid == first
        idx_rows.append(first)
        w_rows.append(jnp.sum(jnp.where(pick, scores, 0.0), axis=0, keepdims=True))
        masked = jnp.where(pick, -jnp.inf, masked)
        chosen = jnp.where(pick, 1.0, chosen)
    gw = jnp.concatenate(w_rows, axis=0)
    gw = gw / jnp.sum(gw, axis=0, keepdims=True) * ROUTED_SCALE
    eidx_ref[...] = jnp.concatenate(idx_rows, axis=0).astype(jnp.int32)
    gw_ref[...] = gw

    @pl.when(pl.program_id(0) == 0)
    def _():
        cnt_ref[...] = jnp.zeros(cnt_ref.shape, F32)

    cnt_ref[...] += jnp.sum(chosen, axis=1, keepdims=True)


def _router(x2, w_router, router_bias, tm):
    t, d = x2.shape
    wr_t = w_router.T.astype(F32)
    wr_hi = wr_t.astype(BF16)
    wr_lo = (wr_t - wr_hi.astype(F32)).astype(BF16)
    return pl.pallas_call(
        _router_kernel, grid=(t // tm,),
        in_specs=[pl.BlockSpec((tm, d), lambda i: (i, 0)),
                  pl.BlockSpec((N_EXPERTS, d), lambda i: (0, 0)),
                  pl.BlockSpec((N_EXPERTS, d), lambda i: (0, 0)),
                  pl.BlockSpec((N_EXPERTS, 1), lambda i: (0, 0))],
        out_specs=[pl.BlockSpec((TOP_K, tm), lambda i: (0, i)),
                   pl.BlockSpec((TOP_K, tm), lambda i: (0, i)),
                   pl.BlockSpec((N_EXPERTS, 1), lambda i: (0, 0))],
        out_shape=[jax.ShapeDtypeStruct((TOP_K, t), jnp.int32),
                   jax.ShapeDtypeStruct((TOP_K, t), F32),
                   jax.ShapeDtypeStruct((N_EXPERTS, 1), F32)],
        compiler_params=_params(1), name="router",
    )(x2, wr_hi, wr_lo, router_bias.reshape(N_EXPERTS, 1).astype(F32))


IDS_WINDOW = 2048
IDS_ALIGN = 1024
X_SLOTS, Y_SLOTS, ID_SLOTS = 4, 4, 8
DMA_GROUPS = 8


def _expert_kernel(bstart_ref, bcount_ref, jwin_ref, joff_ref, nvalid_ref, nused_ref,
                   ids_hbm, x_hbm, w1_ref, w3_ref, w2_ref, y_hbm,
                   ids_smem, xbuf, ybuf, isem, gsem, ssem, *, n_sorted):
    e = pl.program_id(0)
    nused = nused_ref[0]
    rows = MOE_BLOCK
    tpr = w1_ref.shape[0] // LANES
    spare0 = y_hbm.shape[0] - Y_SLOTS * rows

    def x_row(tok):
        return x_hbm.at[pl.ds(pl.multiple_of(tok * tpr, tpr), tpr)]

    def ids_copies(blk, slot):
        src0 = pl.multiple_of(jwin_ref[blk] * IDS_ALIGN, IDS_ALIGN)
        dst0 = pl.multiple_of(slot * 2 * IDS_WINDOW, IDS_WINDOW)
        return [pltpu.make_async_copy(ids_hbm.at[pl.ds(src0 + part * n_sorted, IDS_WINDOW)],
                                      ids_smem.at[pl.ds(dst0 + part * IDS_WINDOW, IDS_WINDOW)],
                                      isem.at[slot]) for part in range(2)]

    def start_ids(blk, slot):
        for c in ids_copies(blk, slot):
            c.start()

    def wait_ids(blk):
        for c in ids_copies(blk, blk % ID_SLOTS):
            c.wait()

    def gather_rows(blk, slot, r0, r1):
        base = (blk % ID_SLOTS) * 2 * IDS_WINDOW + joff_ref[blk]
        for r in range(r0, r1):
            tok = ids_smem[base + r]
            pltpu.make_async_copy(x_row(tok), xbuf.at[slot, pl.ds(r * tpr, tpr)],
                                  gsem.at[slot]).start(priority=r % 2)

    def wait_gather(blk):
        slot = blk % X_SLOTS
        pltpu.make_async_copy(x_hbm.at[pl.ds(0, rows * tpr)], xbuf.at[slot],
                              gsem.at[slot]).wait()

    def scatter_rows(blk, nvalid, slot, r0, r1):
        base = (blk % ID_SLOTS) * 2 * IDS_WINDOW + IDS_WINDOW + joff_ref[blk]
        spare = spare0 + slot * rows
        for r in range(r0, r1):
            dst = jnp.where(r < nvalid, ids_smem[base + r], spare + r)
            pltpu.make_async_copy(ybuf.at[slot, pl.ds(r, 1)], y_hbm.at[pl.ds(dst, 1)],
                                  ssem.at[slot]).start(priority=r % 2)

    def wait_scatter(slot):
        pltpu.make_async_copy(ybuf.at[slot], y_hbm.at[pl.ds(0, rows)], ssem.at[slot]).wait()

    def block_body(phase, g):
        ys, ps, ahead = phase, (phase + Y_SLOTS - 1) % Y_SLOTS, (phase + 2) % X_SLOTS
        prev = jnp.maximum(g - 1, 0)
        nvalid_prev = jnp.where(g > 0, nvalid_ref[prev], 0)
        per = rows // DMA_GROUPS
        groups = iter(range(DMA_GROUPS))

        def dma_group():
            k = next(groups)
            gather_rows(g + 2, ahead, k * per, (k + 1) * per)
            scatter_rows(prev, nvalid_prev, ps, k * per, (k + 1) * per)

        xb = _load_token_tiles(xbuf.at[phase], rows, tpr)
        nh = w1_ref.shape[1] // 2
        parts = []
        for w_ref in (w1_ref, w3_ref):
            for c in range(2):
                parts.append(jnp.dot(xb, w_ref[:, c * nh:(c + 1) * nh],
                                     preferred_element_type=F32))
                dma_group()
        h = _silu(jnp.concatenate(parts[:2], axis=1)) * jnp.concatenate(parts[2:], axis=1)
        ny = w2_ref.shape[1] // 4
        for c in range(4):
            ybuf[ys, :, c * ny:(c + 1) * ny] = jnp.dot(h, w2_ref[:, c * ny:(c + 1) * ny],
                                                       preferred_element_type=F32)
            dma_group()
        return 0

    @pl.when(e == 0)
    def _():
        for blk in range(4):
            start_ids(blk, blk)
        ybuf[...] = jnp.zeros(ybuf.shape, F32)
        for s in range(Y_SLOTS):
            spare = pltpu.make_async_copy(ybuf.at[0], y_hbm.at[pl.ds(spare0 + s * rows, rows)],
                                          ssem.at[s])
            spare.start()
            spare.wait()
        for blk in range(2):
            wait_ids(blk)
            gather_rows(blk, blk, 0, rows)

    def block(j, carry):
        g = bstart_ref[e] + j
        start_ids(g + 4, (g + 4) % ID_SLOTS)
        wait_ids(g + 2)
        wait_gather(g)

        @pl.when(g >= Y_SLOTS - 1)
        def _():
            wait_scatter(g % Y_SLOTS)

        lax.switch(g % X_SLOTS, [functools.partial(block_body, ph) for ph in range(X_SLOTS)], g)
        return carry

    lax.fori_loop(0, bcount_ref[e], block, 0)

    @pl.when(e == pl.num_programs(0) - 1)
    def _():
        last = nused - 1

        def scatter_last(slot, _):
            scatter_rows(last, nvalid_ref[last], slot, 0, rows)
            return 0

        lax.switch(last % Y_SLOTS, [functools.partial(scatter_last, s) for s in range(Y_SLOTS)], 0)
        for s in range(Y_SLOTS - 1):
            @pl.when(s <= last)
            def _():
                wait_scatter(s)
        wait_scatter(Y_SLOTS - 1)
        for blk in range(1, 3):
            wait_gather(last + blk)
        for blk in range(3, 5):
            wait_ids(last + blk)


def _routed_experts(xt, eidx, counts, w1, w3, w2):
    e, d, f = w1.shape
    tpr = d // LANES
    t = xt.shape[0] // tpr
    m = t * TOP_K
    flat_e = eidx.reshape(m)
    _, order = lax.sort_key_val(flat_e, jnp.arange(m, dtype=jnp.int32))
    tok_sorted = order // TOP_K
    dst_sorted = (order % TOP_K) * t + tok_sorted
    n_sorted = m + IDS_WINDOW
    pad = jnp.zeros((IDS_WINDOW,), jnp.int32)
    ids = jnp.concatenate([tok_sorted, pad, dst_sorted, pad])

    counts = counts.reshape(e).astype(jnp.int32)
    ends = jnp.cumsum(counts)
    starts = ends - counts
    pcounts = (counts + MOE_BLOCK - 1) // MOE_BLOCK * MOE_BLOCK
    pends = jnp.cumsum(pcounts)
    pstarts = pends - pcounts
    n_blk = -(-m // MOE_BLOCK) + e
    nused = (pends[-1] // MOE_BLOCK).astype(jnp.int32)
    n_meta = n_blk + ID_SLOTS
    row0 = jnp.arange(n_meta, dtype=jnp.int32) * MOE_BLOCK
    blk_e = jnp.minimum(jnp.sum(row0[:, None] >= pends[None, :], axis=1), e - 1).astype(jnp.int32)
    live = jnp.arange(n_meta) < nused
    onehot = blk_e[:, None] == jnp.arange(e)[None, :]
    pick = lambda v: jnp.sum(jnp.where(onehot, v[None, :], 0), axis=1)
    in_seg = row0 - pick(pstarts)
    jstart = jnp.where(live, pick(starts) + in_seg, 0)
    nvalid = jnp.where(live, jnp.clip(pick(counts) - in_seg, 0, MOE_BLOCK), 0).astype(jnp.int32)
    jwin = (jstart // IDS_ALIGN).astype(jnp.int32)
    joff = (jstart % IDS_ALIGN).astype(jnp.int32)
    bstart = (pstarts // MOE_BLOCK).astype(jnp.int32)
    bcount = (pcounts // MOE_BLOCK).astype(jnp.int32)

    grid_spec = pltpu.PrefetchScalarGridSpec(
        num_scalar_prefetch=6, grid=(e,),
        in_specs=[pl.BlockSpec(memory_space=pl.ANY),
                  pl.BlockSpec(memory_space=pl.ANY),
                  pl.BlockSpec((None, d, f), lambda i, *_: (i, 0, 0)),
                  pl.BlockSpec((None, d, f), lambda i, *_: (i, 0, 0)),
                  pl.BlockSpec((None, f, d), lambda i, *_: (i, 0, 0))],
        out_specs=pl.BlockSpec(memory_space=pl.ANY),
        scratch_shapes=[pltpu.SMEM((ID_SLOTS * 2 * IDS_WINDOW,), jnp.int32),
                        pltpu.VMEM((X_SLOTS, MOE_BLOCK * tpr, LANES), F32),
                        pltpu.VMEM((Y_SLOTS, MOE_BLOCK, d), F32),
                        pltpu.SemaphoreType.DMA((ID_SLOTS,)),
                        pltpu.SemaphoreType.DMA((X_SLOTS,)),
                        pltpu.SemaphoreType.DMA((Y_SLOTS,))])
    return pl.pallas_call(
        functools.partial(_expert_kernel, n_sorted=n_sorted), grid_spec=grid_spec,
        out_shape=jax.ShapeDtypeStruct((m + Y_SLOTS * MOE_BLOCK, d), F32),
        compiler_params=_params(1), name="experts",
    )(bstart, bcount, jwin, joff, nvalid, nused.reshape(1), ids, xt, w1, w3, w2)


def _final_kernel(x_ref, gw_ref, *refs):
    y_refs = refs[:TOP_K]
    ws1_ref, ws3_ref, ws2_ref, g_ref, b_ref, o_ref = refs[TOP_K:]
    x = x_ref[...]
    xb = x.astype(BF16)
    hid = _silu(jnp.dot(xb, ws1_ref[...], preferred_element_type=F32)) * jnp.dot(
        xb, ws3_ref[...], preferred_element_type=F32)
    shared = jnp.dot(hid.astype(BF16), ws2_ref[...], preferred_element_type=F32)
    gw = gw_ref[...]
    routed = gw[:, 0:1] * y_refs[0][...]
    for k in range(1, TOP_K):
        routed = routed + gw[:, k:k + 1] * y_refs[k][...]
    o_ref[...] = _layer_norm(DEEPNORM_ALPHA * x + (routed + shared), g_ref[...], b_ref[...])


def _final(x2, gw, y, ws1, ws3, ws2, ln3_g, ln3_b, tm):
    t, d = x2.shape
    f = ws1.shape[1]
    nrow = t // tm

    def full(shape):
        return pl.BlockSpec(shape, lambda i: (0,) * len(shape))

    y_specs = [pl.BlockSpec((tm, d), functools.partial(lambda i, k: (k * nrow + i, 0), k=k))
               for k in range(TOP_K)]
    return pl.pallas_call(
        _final_kernel, grid=(nrow,),
        in_specs=[pl.BlockSpec((tm, d), lambda i: (i, 0)),
                  pl.BlockSpec((tm, TOP_K), lambda i: (i, 0))] + y_specs +
                 [full((d, f)), full((d, f)), full((f, d)), full((1, d)), full((1, d))],
        out_specs=pl.BlockSpec((tm, d), lambda i: (i, 0)),
        out_shape=jax.ShapeDtypeStruct((t, d), F32),
        compiler_params=_params(1), name="final",
    )(x2, gw, *([y] * TOP_K), ws1.astype(BF16), ws3.astype(BF16), ws2.astype(BF16),
      ln3_g.reshape(1, d), ln3_b.reshape(1, d))


def kernel(x, mem, w_in, sgu_ln_g, sgu_ln_b, w_s, b_s, w_out, ln1_g, ln1_b, w_mq, w_mkv, w_mo,
           ln2_g, ln2_b, w_router, router_bias, w1, w3, w2, ws1, ws3, ws2, ln3_g, ln3_b):
    b, s, d = x.shape
    assert b == 1
    h = x.reshape(s, d)
    mem2d = mem.reshape(mem.shape[1], d)
    for l in range(DEPTH):
        qs, qf, kb, kmean, vb, sgu = _projections(
            h, w_in[l].astype(BF16), sgu_ln_g[l], sgu_ln_b[l], w_s[l], b_s[l], tm=min(1024, s))
        attn = _moba(qs, qf, kb, kmean, vb, ck=4 * MOBA_BLOCK, hs=4)
        x2, x2t = _postmix(attn, sgu, h, mem2d, w_out[l], ln1_g[l], ln1_b[l], w_mq[l], w_mkv[l],
                           w_mo[l], ln2_g[l], ln2_b[l], tm=256)
        eidx_t, gw_t, counts = _router(x2, w_router[l], router_bias[l], tm=min(512, s))
        y = _routed_experts(x2t, eidx_t.T, counts, w1[l], w3[l], w2[l])
        h = _final(x2, gw_t.T, y, ws1[l], ws3[l], ws2[l], ln3_g[l], ln3_b[l], tm=128)
    return h.reshape(b, s, d)
```

```python
import functools

import jax
import jax.numpy as jnp
from jax import lax
from jax.experimental import pallas as pl
from jax.experimental.pallas import tpu as pltpu

F32 = jnp.float32
BF16 = jnp.bfloat16

N_ATTN_HEADS = 8
HEAD_DIM = 128
ATTN_WIDTH = N_ATTN_HEADS * HEAD_DIM
N_SGU_GROUPS = 8
SGU_GROUP_DIM = 128
SGU_WIDTH = N_SGU_GROUPS * SGU_GROUP_DIM
SGU_CHUNK = 128
MOBA_BLOCK = 256
MOBA_TOPK = 3
ROPE_THETA = 500000.0
ROPE_DIM = HEAD_DIM // 4
MEM_HEADS = 4
MEM_HEAD_DIM = 128
N_EXPERTS = 256
TOP_K = 8
N_EXPERT_GROUPS = 8
GROUP_SIZE = N_EXPERTS // N_EXPERT_GROUPS
TOPK_GROUPS = 4
ROUTED_SCALE = 2.5
MOE_BLOCK = 128
LN_EPS = 1e-5
DEPTH = 1
DEEPNORM_ALPHA = (2 * DEPTH) ** 0.25

LANES = 128
MASK_NEG = -1e30
LOG2_E = 1.4426950408889634
VMEM_LIMIT = 56 * 1024 * 1024

_NT = (((1,), (1,)), ((), ()))


def _params(n_axes, vmem=VMEM_LIMIT):
    return pltpu.CompilerParams(dimension_semantics=("arbitrary",) * n_axes,
                                vmem_limit_bytes=vmem)


def _layer_norm(x, g, b):
    mu = jnp.mean(x, axis=-1, keepdims=True)
    xc = x - mu
    var = jnp.mean(xc * xc, axis=-1, keepdims=True)
    return xc * lax.rsqrt(var + LN_EPS) * g + b


def _silu(x):
    return x * jax.nn.sigmoid(x)


def _store_token_tiles(ref, x):
    rows, n = x.shape[0], x.shape[1] // LANES
    for c in range(n):
        ref[pl.ds(c, rows, stride=n), :] = x[:, c * LANES:(c + 1) * LANES]


def _load_token_tiles(ref, rows, n):
    return jnp.concatenate([ref[pl.ds(c, rows, stride=n), :] for c in range(n)], axis=1)


def _rope(acc, cosf, sinf):
    half = ROPE_DIM // 2
    rows = acc.shape[0]
    lane = lax.broadcasted_iota(jnp.int32, (rows, LANES), 1)
    outs = []
    for c in range(acc.shape[1] // LANES):
        t = acc[:, c * LANES:(c + 1) * LANES]
        partner = jnp.where(lane < half, pltpu.roll(t, LANES - half, 1), pltpu.roll(t, half, 1))
        outs.append(t * cosf + partner * sinf)
    return jnp.concatenate(outs, axis=1)


def _qkv_kernel(x_ref, w_ref, cos_ref, sin_ref, qs_ref, qf_ref, k_ref, km_ref, v_ref):
    j = pl.program_id(1)
    acc = jnp.dot(x_ref[...].astype(BF16), w_ref[...], preferred_element_type=F32)
    nq = ATTN_WIDTH // acc.shape[1]

    @pl.when(j < nq)
    def _():
        r = _rope(acc, cos_ref[...], sin_ref[...])
        qf_ref[...] = r
        qs_ref[...] = (r * (HEAD_DIM ** -0.5 * LOG2_E)).astype(BF16)

    @pl.when(jnp.logical_and(j >= nq, j < 2 * nq))
    def _():
        r = _rope(acc, cos_ref[...], sin_ref[...])
        k_ref[...] = r.astype(BF16)
        rows, cols = r.shape
        km_ref[0] = jnp.mean(r.reshape(rows // MOBA_BLOCK, MOBA_BLOCK, cols), axis=1)

    @pl.when(j >= 2 * nq)
    def _():
        v_ref[...] = acc.astype(BF16)


def _sgu_kernel(x_ref, wu_ref, wz_ref, g_ref, b_ref, ws_ref, bs_ref, o_ref):
    xb = x_ref[...].astype(BF16)
    u = jax.nn.gelu(jnp.dot(xb, wu_ref[...], preferred_element_type=F32))
    z = jax.nn.gelu(jnp.dot(xb, wz_ref[...], preferred_element_type=F32))
    rows, cols = u.shape
    t_row = lax.broadcasted_iota(jnp.int32, (SGU_CHUNK, SGU_CHUNK), 0)
    s_col = lax.broadcasted_iota(jnp.int32, (SGU_CHUNK, SGU_CHUNK), 1)
    causal = s_col <= t_row
    for gl in range(cols // SGU_GROUP_DIM):
        sl = slice(gl * SGU_GROUP_DIM, (gl + 1) * SGU_GROUP_DIM)
        zn = _layer_norm(z[:, sl], g_ref[:, sl], b_ref[:, sl]).astype(BF16)
        w = jnp.where(causal, ws_ref[gl], 0.0).astype(BF16)
        bias = bs_ref[:, gl:gl + 1]
        for c in range(rows // SGU_CHUNK):
            rs = slice(c * SGU_CHUNK, (c + 1) * SGU_CHUNK)
            mixed = jnp.dot(w, zn[rs], preferred_element_type=F32) + bias
            o_ref[rs, sl] = (u[rs, sl] * mixed).astype(BF16)


def _projections(x2d, w_in_bf, sgu_ln_g, sgu_ln_b, w_s, b_s, tm):
    s, d = x2d.shape
    tn = 512
    nrow = s // tm
    half = ROPE_DIM // 2

    pos = jnp.arange(s)
    inv_freq = ROPE_THETA ** (-jnp.arange(half, dtype=F32) / half)
    ang = pos.astype(F32)[:, None] * inv_freq[None, :]
    cos, sin = jnp.cos(ang), jnp.sin(ang)
    rest = LANES - ROPE_DIM
    cosf = jnp.concatenate([cos, cos, jnp.ones((s, rest), F32)], axis=1)
    sinf = jnp.concatenate([-sin, sin, jnp.zeros((s, rest), F32)], axis=1)

    x_spec = pl.BlockSpec((tm, d), lambda i, j: (i, 0))
    tab_spec = pl.BlockSpec((tm, LANES), lambda i, j: (i, 0))
    out_spec = pl.BlockSpec((tm, tn), lambda i, j: (i, j))
    nq = ATTN_WIDTH // tn

    def w_spec(col0):
        return pl.BlockSpec((d, tn), lambda i, j: (0, col0 // tn + j))

    def part_spec(part):
        return pl.BlockSpec((tm, tn), lambda i, j: (i, jnp.clip(j - part * nq, 0, nq - 1)))

    kpb = tm // MOBA_BLOCK
    qs, qf, kb, kmean, vb = pl.pallas_call(
        _qkv_kernel, grid=(nrow, 3 * nq),
        in_specs=[x_spec, w_spec(0), tab_spec, tab_spec],
        out_specs=[part_spec(0), part_spec(0), part_spec(1),
                   pl.BlockSpec((1, kpb, tn), lambda i, j: (i, 0, jnp.clip(j - nq, 0, nq - 1))),
                   part_spec(2)],
        out_shape=[jax.ShapeDtypeStruct((s, ATTN_WIDTH), BF16),
                   jax.ShapeDtypeStruct((s, ATTN_WIDTH), F32),
                   jax.ShapeDtypeStruct((s, ATTN_WIDTH), BF16),
                   jax.ShapeDtypeStruct((nrow, kpb, ATTN_WIDTH), F32),
                   jax.ShapeDtypeStruct((s, ATTN_WIDTH), BF16)],
        compiler_params=_params(2), name="proj_qkv",
    )(x2d, w_in_bf, cosf, sinf)
    kmean = kmean.reshape(s // MOBA_BLOCK, ATTN_WIDTH)

    gpb = tn // SGU_GROUP_DIM
    sgu = pl.pallas_call(
        _sgu_kernel, grid=(nrow, SGU_WIDTH // tn),
        in_specs=[x_spec, w_spec(3 * ATTN_WIDTH), w_spec(3 * ATTN_WIDTH + SGU_WIDTH),
                  pl.BlockSpec((1, tn), lambda i, j: (0, j)),
                  pl.BlockSpec((1, tn), lambda i, j: (0, j)),
                  pl.BlockSpec((gpb, SGU_CHUNK, SGU_CHUNK), lambda i, j: (j, 0, 0)),
                  pl.BlockSpec((None, SGU_CHUNK, gpb), lambda i, j: (j, 0, 0))],
        out_specs=out_spec,
        out_shape=jax.ShapeDtypeStruct((s, SGU_WIDTH), BF16),
        compiler_params=_params(2), name="proj_sgu",
    )(x2d, w_in_bf, w_in_bf, sgu_ln_g.reshape(1, SGU_WIDTH), sgu_ln_b.reshape(1, SGU_WIDTH),
      w_s, b_s.reshape(N_SGU_GROUPS // gpb, gpb, SGU_CHUNK).transpose(0, 2, 1))
    return qs, qf, kb, kmean, vb, sgu


def _moba_kernel(qs_ref, qf_ref, km_ref, oh_ref, k_ref, v_ref, o_ref,
                 qaug_scr, sa_scr, sb_scr, m_scr, acc_scr, *, tq, ck, hs):
    i = pl.program_id(1)
    own = i
    n_full = own // (ck // MOBA_BLOCK)
    col = lax.broadcasted_iota(jnp.int32, (tq, LANES), 1)
    colf = col.astype(F32)

    for hh in range(hs):
        hsl = slice(hh * HEAD_DIM, (hh + 1) * HEAD_DIM)
        gate = lax.dot_general(qf_ref[:, hsl], km_ref[:, hsl], _NT,
                               precision=lax.Precision.HIGHEST,
                               preferred_element_type=F32)
        gate = jnp.where(col < own, gate, -jnp.inf)
        selbias = jnp.where(col == own, 0.0, MASK_NEG)
        for _ in range(MOBA_TOPK):
            gmax = jnp.max(gate, axis=-1, keepdims=True)
            cand = jnp.where(gate == gmax, colf, float(LANES))
            cand = jnp.where(gmax > -jnp.inf, cand, float(LANES))
            first = jnp.min(cand, axis=-1, keepdims=True)
            pick = colf == first
            selbias = jnp.where(pick, 0.0, selbias)
            gate = jnp.where(pick, -jnp.inf, gate)
        qaug_scr[hh] = jnp.concatenate([qs_ref[:, hsl], selbias.astype(BF16)], axis=1)

    def scores(c0, s_ref, causal=None):
        c0 = pl.multiple_of(c0, ck)
        onehot = oh_ref[pl.ds(c0, ck), :]
        for hh in range(hs):
            hsl = slice(hh * HEAD_DIM, (hh + 1) * HEAD_DIM)
            kaug = jnp.concatenate([k_ref[pl.ds(c0, ck), hsl], onehot], axis=1)
            sc = lax.dot_general(qaug_scr[hh], kaug, _NT, preferred_element_type=F32)
            if causal is not None:
                sc = jnp.where(causal, sc, MASK_NEG)
            s_ref[hh] = sc

    ones = jnp.ones((ck, HEAD_DIM), BF16)

    def absorb(c0, s_ref):
        c0 = pl.multiple_of(c0, ck)
        for hh in range(hs):
            hsl = slice(hh * HEAD_DIM, (hh + 1) * HEAD_DIM)
            sc = s_ref[hh]
            m_old = m_scr[hh]
            m_new = jnp.maximum(m_old, jnp.max(sc, axis=-1, keepdims=True))
            alpha = jnp.exp2(m_old - m_new)
            pn = jnp.exp2(sc - m_new[:, :1])
            v1 = jnp.concatenate([v_ref[pl.ds(c0, ck), hsl], ones], axis=1)
            acc_scr[hh] = jnp.concatenate([alpha, alpha], axis=1) * acc_scr[hh] + jnp.dot(
                pn.astype(BF16), v1, preferred_element_type=F32)
            m_scr[hh] = m_new

    m_scr[...] = jnp.full(m_scr.shape, MASK_NEG, F32)
    acc_scr[...] = jnp.zeros(acc_scr.shape, F32)
    d0 = n_full * ck
    q_pos = i * tq + lax.broadcasted_iota(jnp.int32, (tq, ck), 0)
    k_pos = d0 + lax.broadcasted_iota(jnp.int32, (tq, ck), 1)
    scores(d0, sa_scr, k_pos <= q_pos)

    def pair(u, carry):
        prev0 = jnp.where(u == 0, d0, (2 * u - 1) * ck)
        absorb(prev0, sa_scr)
        scores(2 * u * ck, sb_scr)
        absorb(2 * u * ck, sb_scr)
        scores((2 * u + 1) * ck, sa_scr)
        return carry

    n_pairs = n_full // 2
    lax.fori_loop(0, n_pairs, pair, 0)
    last0 = jnp.where(n_pairs == 0, d0, (2 * n_pairs - 1) * ck)

    @pl.when(n_full % 2 == 1)
    def _():
        absorb(last0, sa_scr)
        scores((n_full - 1) * ck, sb_scr)
        absorb((n_full - 1) * ck, sb_scr)

    @pl.when(n_full % 2 == 0)
    def _():
        absorb(last0, sa_scr)

    for hh in range(hs):
        acc = acc_scr[hh]
        o_ref[:, hh * HEAD_DIM:(hh + 1) * HEAD_DIM] = (
            acc[:, :HEAD_DIM] / acc[:, HEAD_DIM:]).astype(BF16)


def _moba(qs, qf, kb, kmean, vb, ck, hs):
    s = qs.shape[0]
    tq = MOBA_BLOCK
    nb = s // MOBA_BLOCK
    assert nb <= LANES and s % ck == 0 and ck % MOBA_BLOCK == 0 and N_ATTN_HEADS % hs == 0
    km_pad = jnp.zeros((LANES, ATTN_WIDTH), F32).at[:nb].set(kmean)
    onehot = (jnp.arange(s)[:, None] // MOBA_BLOCK == jnp.arange(LANES)[None, :]).astype(BF16)
    gw = hs * HEAD_DIM
    tile = pl.BlockSpec((tq, gw), lambda g, i: (i, g))
    once = pl.Buffered(1)
    return pl.pallas_call(
        functools.partial(_moba_kernel, tq=tq, ck=ck, hs=hs),
        grid=(N_ATTN_HEADS // hs, s // tq),
        in_specs=[tile, tile,
                  pl.BlockSpec((LANES, gw), lambda g, i: (0, g)),
                  pl.BlockSpec((s, LANES), lambda g, i: (0, 0), pipeline_mode=once),
                  pl.BlockSpec((s, gw), lambda g, i: (0, g), pipeline_mode=once),
                  pl.BlockSpec((s, gw), lambda g, i: (0, g), pipeline_mode=once)],
        out_specs=tile,
        out_shape=jax.ShapeDtypeStruct((s, ATTN_WIDTH), BF16),
        scratch_shapes=[pltpu.VMEM((hs, tq, 2 * HEAD_DIM), BF16)] +
                       [pltpu.VMEM((hs, tq, ck), F32)] * 2 +
                       [pltpu.VMEM((hs, tq, LANES), F32),
                        pltpu.VMEM((hs, tq, 2 * HEAD_DIM), F32)],
        compiler_params=_params(2), name="moba",
    )(qs, qf, km_pad, onehot, kb, vb)


def _memkv_kernel(mem_ref, w_ref, kv_ref):
    kv_ref[...] = jnp.dot(mem_ref[...].astype(BF16), w_ref[...],
                          preferred_element_type=F32).astype(BF16)


def _postmix_kernel(attn_ref, sgu_ref, x_ref, wout_ref, g1_ref, b1_ref, wmq_ref, kv_ref,
                    wmo_ref, g2_ref, b2_ref, o_ref, op_ref):
    mixin = jnp.concatenate([attn_ref[...], sgu_ref[...]], axis=1)
    mix = jnp.dot(mixin, wout_ref[...], preferred_element_type=F32)
    y1 = _layer_norm(DEEPNORM_ALPHA * x_ref[...] + mix, g1_ref[...], b1_ref[...])

    q = jnp.dot(y1.astype(BF16), wmq_ref[...], preferred_element_type=F32)
    q = (q * (MEM_HEAD_DIM ** -0.5)).astype(BF16)
    width = MEM_HEADS * MEM_HEAD_DIM
    heads = []
    for h in range(MEM_HEADS):
        sl = slice(h * MEM_HEAD_DIM, (h + 1) * MEM_HEAD_DIM)
        sc = lax.dot_general(q[:, sl], kv_ref[:, sl], _NT, preferred_element_type=F32)
        p = jnp.exp(sc - jnp.max(sc, axis=-1, keepdims=True))
        denom = jnp.sum(p, axis=-1, keepdims=True)
        vsl = slice(width + h * MEM_HEAD_DIM, width + (h + 1) * MEM_HEAD_DIM)
        heads.append(jnp.dot(p.astype(BF16), kv_ref[:, vsl], preferred_element_type=F32) / denom)
    o = jnp.concatenate(heads, axis=1).astype(BF16)
    xat = jnp.dot(o, wmo_ref[...], preferred_element_type=F32)
    y2 = _layer_norm(DEEPNORM_ALPHA * y1 + xat, g2_ref[...], b2_ref[...])
    o_ref[...] = y2
    _store_token_tiles(op_ref, y2)


def _postmix(attn, sgu, x2d, mem2d, w_out, ln1_g, ln1_b, w_mq, w_mkv, w_mo, ln2_g, ln2_b, tm):
    s, d = x2d.shape
    m = mem2d.shape[0]
    width = MEM_HEADS * MEM_HEAD_DIM
    kv = pl.pallas_call(
        _memkv_kernel,
        out_shape=jax.ShapeDtypeStruct((m, 2 * width), BF16),
        compiler_params=_params(0), name="mem_kv",
    )(mem2d, w_mkv.astype(BF16))

    def full(shape):
        return pl.BlockSpec(shape, lambda i: (0,) * len(shape))

    def rows(cols):
        return pl.BlockSpec((tm, cols), lambda i: (i, 0))

    return pl.pallas_call(
        _postmix_kernel, grid=(s // tm,),
        in_specs=[rows(ATTN_WIDTH), rows(SGU_WIDTH), rows(d), full((ATTN_WIDTH + SGU_WIDTH, d)),
                  full((1, d)), full((1, d)), full((d, width)), full((m, 2 * width)),
                  full((width, d)), full((1, d)), full((1, d))],
        out_specs=[rows(d), pl.BlockSpec((tm * (d // LANES), LANES), lambda i: (i, 0))],
        out_shape=[jax.ShapeDtypeStruct((s, d), F32),
                   jax.ShapeDtypeStruct((s * (d // LANES), LANES), F32)],
        compiler_params=_params(1), name="postmix",
    )(attn, sgu, x2d, w_out.astype(BF16), ln1_g.reshape(1, d), ln1_b.reshape(1, d),
      w_mq.astype(BF16), kv, w_mo.astype(BF16), ln2_g.reshape(1, d), ln2_b.reshape(1, d))


def _router_kernel(x_ref, wh_ref, wl_ref, bias_ref, eidx_ref, gw_ref, cnt_ref):
    tm = x_ref.shape[0]
    x = x_ref[...]
    xh = x.astype(BF16)
    xl = (x - xh.astype(F32)).astype(BF16)
    wh = wh_ref[...]
    logits = (lax.dot_general(wh, xh, _NT, preferred_element_type=F32)
              + lax.dot_general(wh, xl, _NT, preferred_element_type=F32)
              + lax.dot_general(wl_ref[...], xh, _NT, preferred_element_type=F32))
    scores = jax.nn.sigmoid(logits)
    biased = scores + bias_ref[...]

    g3 = biased.reshape(N_EXPERT_GROUPS, GROUP_SIZE, tm)
    in_grp = lax.broadcasted_iota(jnp.int32, g3.shape, 1).astype(F32)
    top1 = jnp.max(g3, axis=1, keepdims=True)
    first = jnp.min(jnp.where(g3 == top1, in_grp, float(GROUP_SIZE)), axis=1, keepdims=True)
    top2 = jnp.max(jnp.where(in_grp == first, -jnp.inf, g3), axis=1, keepdims=True)
    grp_score = (top1 + top2).reshape(N_EXPERT_GROUPS, tm)

    gid = lax.broadcasted_iota(jnp.int32, grp_score.shape, 0).astype(F32)
    grp_keep = jnp.zeros(grp_score.shape, F32)
    for _ in range(TOPK_GROUPS):
        best = jnp.max(grp_score, axis=0, keepdims=True)
        first = jnp.min(jnp.where(grp_score == best, gid, float(N_EXPERT_GROUPS)),
                        axis=0, keepdims=True)
        pick = gid == first
        grp_keep = jnp.where(pick, 1.0, grp_keep)
        grp_score = jnp.where(pick, -jnp.inf, grp_score)
    keep = jnp.broadcast_to(grp_keep.reshape(N_EXPERT_GROUPS, 1, tm), g3.shape) > 0.5
    masked = jnp.where(keep, g3, -jnp.inf).reshape(N_EXPERTS, tm)

    eid = lax.broadcasted_iota(jnp.int32, masked.shape, 0).astype(F32)
    idx_rows, w_rows = [], []
    chosen = jnp.zeros(masked.shape, F32)
    for _ in range(TOP_K):
        best = jnp.max(masked, axis=0, keepdims=True)
        first = jnp.min(jnp.where(masked == best, eid, float(N_EXPERTS)), axis=0, keepdims=True)
        pick = eid == first
        idx_rows.append(first)
        w_rows.append(jnp.sum(jnp.where(pick, scores, 0.0), axis=0, keepdims=True))
        masked = jnp.where(pick, -jnp.inf, masked)
        chosen = jnp.where(pick, 1.0, chosen)
    gw = jnp.concatenate(w_rows, axis=0)
    gw = gw / jnp.sum(gw, axis=0, keepdims=True) * ROUTED_SCALE
    eidx_ref[...] = jnp.concatenate(idx_rows, axis=0).astype(jnp.int32)
    gw_ref[...] = gw

    @pl.when(pl.program_id(0) == 0)
    def _():
        cnt_ref[...] = jnp.zeros(cnt_ref.shape, F32)

    cnt_ref[...] += jnp.sum(chosen, axis=1, keepdims=True)


def _router(x2, w_router, router_bias, tm):
    t, d = x2.shape
    wr_t = w_router.T.astype(F32)
    wr_hi = wr_t.astype(BF16)
    wr_lo = (wr_t - wr_hi.astype(F32)).astype(BF16)
    return pl.pallas_call(
        _router_kernel, grid=(t // tm,),
        in_specs=[pl.BlockSpec((tm, d), lambda i: (i, 0)),
                  pl.BlockSpec((N_EXPERTS, d), lambda i: (0, 0)),
                  pl.BlockSpec((N_EXPERTS, d), lambda i: (0, 0)),
                  pl.BlockSpec((N_EXPERTS, 1), lambda i: (0, 0))],
        out_specs=[pl.BlockSpec((TOP_K, tm), lambda i: (0, i)),
                   pl.BlockSpec((TOP_K, tm), lambda i: (0, i)),
                   pl.BlockSpec((N_EXPERTS, 1), lambda i: (0, 0))],
        out_shape=[jax.ShapeDtypeStruct((TOP_K, t), jnp.int32),
                   jax.ShapeDtypeStruct((TOP_K, t), F32),
                   jax.ShapeDtypeStruct((N_EXPERTS, 1), F32)],
        compiler_params=_params(1), name="router",
    )(x2, wr_hi, wr_lo, router_bias.reshape(N_EXPERTS, 1).astype(F32))


IDS_WINDOW = 2048
IDS_ALIGN = 1024
X_SLOTS, Y_SLOTS, ID_SLOTS = 4, 4, 8
DMA_GROUPS = 8


def _expert_kernel(bstart_ref, bcount_ref, jwin_ref, joff_ref, nvalid_ref, nused_ref,
                   ids_hbm, x_hbm, w1_ref, w3_ref, w2_ref, y_hbm,
                   ids_smem, xbuf, ybuf, isem, gsem, ssem, *, n_sorted):
    e = pl.program_id(0)
    nused = nused_ref[0]
    rows = MOE_BLOCK
    tpr = w1_ref.shape[0] // LANES
    spare0 = y_hbm.shape[0] - Y_SLOTS * rows

    def x_row(tok):
        return x_hbm.at[pl.ds(pl.multiple_of(tok * tpr, tpr), tpr)]

    def ids_copies(blk, slot):
        src0 = pl.multiple_of(jwin_ref[blk] * IDS_ALIGN, IDS_ALIGN)
        dst0 = pl.multiple_of(slot * 2 * IDS_WINDOW, IDS_WINDOW)
        return [pltpu.make_async_copy(ids_hbm.at[pl.ds(src0 + part * n_sorted, IDS_WINDOW)],
                                      ids_smem.at[pl.ds(dst0 + part * IDS_WINDOW, IDS_WINDOW)],
                                      isem.at[slot]) for part in range(2)]

    def start_ids(blk, slot):
        for c in ids_copies(blk, slot):
            c.start()

    def wait_ids(blk):
        for c in ids_copies(blk, blk % ID_SLOTS):
            c.wait()

    def gather_rows(blk, slot, r0, r1):
        base = (blk % ID_SLOTS) * 2 * IDS_WINDOW + joff_ref[blk]
        for r in range(r0, r1):
            tok = ids_smem[base + r]
            pltpu.make_async_copy(x_row(tok), xbuf.at[slot, pl.ds(r * tpr, tpr)],
                                  gsem.at[slot]).start(priority=r % 2)

    def wait_gather(blk):
        slot = blk % X_SLOTS
        pltpu.make_async_copy(x_hbm.at[pl.ds(0, rows * tpr)], xbuf.at[slot],
                              gsem.at[slot]).wait()

    def scatter_rows(blk, nvalid, slot, r0, r1):
        base = (blk % ID_SLOTS) * 2 * IDS_WINDOW + IDS_WINDOW + joff_ref[blk]
        spare = spare0 + slot * rows
        for r in range(r0, r1):
            dst = jnp.where(r < nvalid, ids_smem[base + r], spare + r)
            pltpu.make_async_copy(ybuf.at[slot, pl.ds(r, 1)], y_hbm.at[pl.ds(dst, 1)],
                                  ssem.at[slot]).start(priority=r % 2)

    def wait_scatter(slot):
        pltpu.make_async_copy(ybuf.at[slot], y_hbm.at[pl.ds(0, rows)], ssem.at[slot]).wait()

    def block_body(phase, g):
        ys, ps, ahead = phase, (phase + Y_SLOTS - 1) % Y_SLOTS, (phase + 2) % X_SLOTS
        prev = jnp.maximum(g - 1, 0)
        nvalid_prev = jnp.where(g > 0, nvalid_ref[prev], 0)
        per = rows // DMA_GROUPS
        groups = iter(range(DMA_GROUPS))

        def dma_group():
            k = next(groups)
            gather_rows(g + 2, ahead, k * per, (k + 1) * per)
            scatter_rows(prev, nvalid_prev, ps, k * per, (k + 1) * per)

        xb = _load_token_tiles(xbuf.at[phase], rows, tpr)
        nh = w1_ref.shape[1] // 2
        parts = []
        for w_ref in (w1_ref, w3_ref):
            for c in range(2):
                parts.append(jnp.dot(xb, w_ref[:, c * nh:(c + 1) * nh],
                                     preferred_element_type=F32))
                dma_group()
        h = _silu(jnp.concatenate(parts[:2], axis=1)) * jnp.concatenate(parts[2:], axis=1)
        ny = w2_ref.shape[1] // 4
        for c in range(4):
            ybuf[ys, :, c * ny:(c + 1) * ny] = jnp.dot(h, w2_ref[:, c * ny:(c + 1) * ny],
                                                       preferred_element_type=F32)
            dma_group()
        return 0

    @pl.when(e == 0)
    def _():
        for blk in range(4):
            start_ids(blk, blk)
        ybuf[...] = jnp.zeros(ybuf.shape, F32)
        for s in range(Y_SLOTS):
            spare = pltpu.make_async_copy(ybuf.at[0], y_hbm.at[pl.ds(spare0 + s * rows, rows)],
                                          ssem.at[s])
            spare.start()
            spare.wait()
        for blk in range(2):
            wait_ids(blk)
            gather_rows(blk, blk, 0, rows)

    def block(j, carry):
        g = bstart_ref[e] + j
        start_ids(g + 4, (g + 4) % ID_SLOTS)
        wait_ids(g + 2)
        wait_gather(g)

        @pl.when(g >= Y_SLOTS - 1)
        def _():
            wait_scatter(g % Y_SLOTS)

        lax.switch(g % X_SLOTS, [functools.partial(block_body, ph) for ph in range(X_SLOTS)], g)
        return carry

    lax.fori_loop(0, bcount_ref[e], block, 0)

    @pl.when(e == pl.num_programs(0) - 1)
    def _():
        last = nused - 1

        def scatter_last(slot, _):
            scatter_rows(last, nvalid_ref[last], slot, 0, rows)
            return 0

        lax.switch(last % Y_SLOTS, [functools.partial(scatter_last, s) for s in range(Y_SLOTS)], 0)
        for s in range(Y_SLOTS - 1):
            @pl.when(s <= last)
            def _():
                wait_scatter(s)
        wait_scatter(Y_SLOTS - 1)
        for blk in range(1, 3):
            wait_gather(last + blk)
        for blk in range(3, 5):
            wait_ids(last + blk)


def _routed_experts(xt, eidx, counts, w1, w3, w2):
    e, d, f = w1.shape
    tpr = d // LANES
    t = xt.shape[0] // tpr
    m = t * TOP_K
    flat_e = eidx.reshape(m)
    _, order = lax.sort_key_val(flat_e, jnp.arange(m, dtype=jnp.int32))
    tok_sorted = order // TOP_K
    dst_sorted = (order % TOP_K) * t + tok_sorted
    n_sorted = m + IDS_WINDOW
    pad = jnp.zeros((IDS_WINDOW,), jnp.int32)
    ids = jnp.concatenate([tok_sorted, pad, dst_sorted, pad])

    counts = counts.reshape(e).astype(jnp.int32)
    ends = jnp.cumsum(counts)
    starts = ends - counts
    pcounts = (counts + MOE_BLOCK - 1) // MOE_BLOCK * MOE_BLOCK
    pends = jnp.cumsum(pcounts)
    pstarts = pends - pcounts
    n_blk = -(-m // MOE_BLOCK) + e
    nused = (pends[-1] // MOE_BLOCK).astype(jnp.int32)
    n_meta = n_blk + ID_SLOTS
    row0 = jnp.arange(n_meta, dtype=jnp.int32) * MOE_BLOCK
    blk_e = jnp.minimum(jnp.sum(row0[:, None] >= pends[None, :], axis=1), e - 1).astype(jnp.int32)
    live = jnp.arange(n_meta) < nused
    onehot = blk_e[:, None] == jnp.arange(e)[None, :]
    pick = lambda v: jnp.sum(jnp.where(onehot, v[None, :], 0), axis=1)
    in_seg = row0 - pick(pstarts)
    jstart = jnp.where(live, pick(starts) + in_seg, 0)
    nvalid = jnp.where(live, jnp.clip(pick(counts) - in_seg, 0, MOE_BLOCK), 0).astype(jnp.int32)
    jwin = (jstart // IDS_ALIGN).astype(jnp.int32)
    joff = (jstart % IDS_ALIGN).astype(jnp.int32)
    bstart = (pstarts // MOE_BLOCK).astype(jnp.int32)
    bcount = (pcounts // MOE_BLOCK).astype(jnp.int32)

    grid_spec = pltpu.PrefetchScalarGridSpec(
        num_scalar_prefetch=6, grid=(e,),
        in_specs=[pl.BlockSpec(memory_space=pl.ANY),
                  pl.BlockSpec(memory_space=pl.ANY),
                  pl.BlockSpec((None, d, f), lambda i, *_: (i, 0, 0)),
                  pl.BlockSpec((None, d, f), lambda i, *_: (i, 0, 0)),
                  pl.BlockSpec((None, f, d), lambda i, *_: (i, 0, 0))],
        out_specs=pl.BlockSpec(memory_space=pl.ANY),
        scratch_shapes=[pltpu.SMEM((ID_SLOTS * 2 * IDS_WINDOW,), jnp.int32),
                        pltpu.VMEM((X_SLOTS, MOE_BLOCK * tpr, LANES), F32),
                        pltpu.VMEM((Y_SLOTS, MOE_BLOCK, d), F32),
                        pltpu.SemaphoreType.DMA((ID_SLOTS,)),
                        pltpu.SemaphoreType.DMA((X_SLOTS,)),
                        pltpu.SemaphoreType.DMA((Y_SLOTS,))])
    return pl.pallas_call(
        functools.partial(_expert_kernel, n_sorted=n_sorted), grid_spec=grid_spec,
        out_shape=jax.ShapeDtypeStruct((m + Y_SLOTS * MOE_BLOCK, d), F32),
        compiler_params=_params(1), name="experts",
    )(bstart, bcount, jwin, joff, nvalid, nused.reshape(1), ids, xt, w1, w3, w2)


def _final_kernel(x_ref, gw_ref, *refs):
    y_refs = refs[:TOP_K]
    ws1_ref, ws3_ref, ws2_ref, g_ref, b_ref, o_ref = refs[TOP_K:]
    x = x_ref[...]
    xb = x.astype(BF16)
    hid = _silu(jnp.dot(xb, ws1_ref[...], preferred_element_type=F32)) * jnp.dot(
        xb, ws3_ref[...], preferred_element_type=F32)
    shared = jnp.dot(hid.astype(BF16), ws2_ref[...], preferred_element_type=F32)
    gw = gw_ref[...]
    routed = gw[:, 0:1] * y_refs[0][...]
    for k in range(1, TOP_K):
        routed = routed + gw[:, k:k + 1] * y_refs[k][...]
    o_ref[...] = _layer_norm(DEEPNORM_ALPHA * x + (routed + shared), g_ref[...], b_ref[...])


def _final(x2, gw, y, ws1, ws3, ws2, ln3_g, ln3_b, tm):
    t, d = x2.shape
    f = ws1.shape[1]
    nrow = t // tm

    def full(shape):
        return pl.BlockSpec(shape, lambda i: (0,) * len(shape))

    y_specs = [pl.BlockSpec((tm, d), functools.partial(lambda i, k: (k * nrow + i, 0), k=k))
               for k in range(TOP_K)]
    return pl.pallas_call(
        _final_kernel, grid=(nrow,),
        in_specs=[pl.BlockSpec((tm, d), lambda i: (i, 0)),
                  pl.BlockSpec((tm, TOP_K), lambda i: (i, 0))] + y_specs +
                 [full((d, f)), full((d, f)), full((f, d)), full((1, d)), full((1, d))],
        out_specs=pl.BlockSpec((tm, d), lambda i: (i, 0)),
        out_shape=jax.ShapeDtypeStruct((t, d), F32),
        compiler_params=_params(1), name="final",
    )(x2, gw, *([y] * TOP_K), ws1.astype(BF16), ws3.astype(BF16), ws2.astype(BF16),
      ln3_g.reshape(1, d), ln3_b.reshape(1, d))


def kernel(x, mem, w_in, sgu_ln_g, sgu_ln_b, w_s, b_s, w_out, ln1_g, ln1_b, w_mq, w_mkv, w_mo,
           ln2_g, ln2_b, w_router, router_bias, w1, w3, w2, ws1, ws3, ws2, ln3_g, ln3_b):
    b, s, d = x.shape
    assert b == 1
    h = x.reshape(s, d)
    mem2d = mem.reshape(mem.shape[1], d)
    for l in range(DEPTH):
        qs, qf, kb, kmean, vb, sgu = _projections(
            h, w_in[l].astype(BF16), sgu_ln_g[l], sgu_ln_b[l], w_s[l], b_s[l], tm=min(1024, s))
        attn = _moba(qs, qf, kb, kmean, vb, ck=4 * MOBA_BLOCK, hs=4)
        x2, x2t = _postmix(attn, sgu, h, mem2d, w_out[l], ln1_g[l], ln1_b[l], w_mq[l], w_mkv[l],
                           w_mo[l], ln2_g[l], ln2_b[l], tm=256)
        eidx_t, gw_t, counts = _router(x2, w_router[l], router_bias[l], tm=min(512, s))
        y = _routed_experts(x2t, eidx_t.T, counts, w1[l], w3[l], w2[l])
        h = _final(x2, gw_t.T, y, ws1[l], ws3[l], ws2[l], ln3_g[l], ln3_b[l], tm=128)
    return h.reshape(b, s, d)
```
